```python
import math
import jax, jax.numpy as jnp
from jax import lax
import numpy as np

D_MODEL = 2048
BATCH = 4
SEQ = 4096
DEPTH = 1

D_MIX = D_MODEL
ATTN_WIDTH = D_MIX // 2
SSM_WIDTH = D_MIX - ATTN_WIDTH
N_HEADS = 8
HEAD_DIM = ATTN_WIDTH // N_HEADS
MOBA_BLOCK = 256
MOBA_TOPK = 3
Q_CHUNK = 32
SSM_GROUP = 16
N_SSM_GROUPS = SSM_WIDTH // SSM_GROUP
SSM_STATE = 64
DT_MIN = 1e-3
DT_MAX = 1e-1
D_FF = ((8 * D_MODEL // 3 + 255) // 256) * 256
IN_PROJ_WIDTH = 3 * ATTN_WIDTH + SSM_WIDTH
N_MOD = 9
LN_EPS = 1e-5
NEG_INF = -1e30
ALPHA = (2.0 * DEPTH) ** 0.25
BETA = (8.0 * DEPTH) ** -0.25

kernel_name = 'hybrid_moba_s5_macaron_block'


def _layer_norm(y, g, b):
    y32 = y.astype(jnp.float32)
    mu = jnp.mean(y32, axis=-1, keepdims=True)
    var = jnp.mean(jnp.square(y32 - mu), axis=-1, keepdims=True)
    return ((y32 - mu) * lax.rsqrt(var + LN_EPS) * g + b).astype(y.dtype)


def _rms_norm(y, g):
    y32 = y.astype(jnp.float32)
    return (y32 * lax.rsqrt(jnp.mean(jnp.square(y32), axis=-1, keepdims=True) + LN_EPS) * g).astype(y.dtype)


def _modulate(y, shift, scale):
    return y * (1.0 + scale) + shift


def _swiglu(u, w_gate, w_up, w_down):
    return (jax.nn.silu(u @ w_gate) * (u @ w_up)) @ w_down


def _alibi_slopes(n_heads):
    return 2.0 ** (-(8.0 / n_heads) * jnp.arange(1, n_heads + 1, dtype=jnp.float32))


def _moba_attention(q, k, v):
    B, S, H, dh = q.shape
    BH = B * H

    def heads_first(t):
        return t.transpose(0, 2, 1, 3).reshape(BH, S, dh)

    q, k, v = heads_first(q), heads_first(k), heads_first(v)
    n_blocks = -(-S // MOBA_BLOCK)
    s_pad = n_blocks * MOBA_BLOCK
    pad = ((0, 0), (0, s_pad - S), (0, 0))
    k_blocks = jnp.pad(k, pad).reshape(BH, n_blocks, MOBA_BLOCK, dh)
    v_blocks = jnp.pad(v, pad).reshape(BH, n_blocks, MOBA_BLOCK, dh)

    k_mean = jnp.mean(k_blocks.astype(jnp.float32), axis=2)
    gate = jnp.einsum('nsd,njd->nsj', q.astype(jnp.float32), k_mean)
    q_block = jnp.arange(S) // MOBA_BLOCK
    fully_past = jnp.arange(n_blocks)[None, :] < q_block[:, None]
    gate = jnp.where(fully_past[None], gate, NEG_INF)
    top_k = min(MOBA_TOPK, n_blocks)
    _, sel = lax.top_k(gate, top_k)
    sel_valid = sel < q_block[None, :, None]

    slopes_n = jnp.tile(_alibi_slopes(H), B)
    scale = dh ** -0.5
    n_chunks = S // Q_CHUNK

    def to_chunks(t):
        return t.reshape((BH, n_chunks, Q_CHUNK) + t.shape[2:]).swapaxes(0, 1)

    q_c, sel_c, valid_c = to_chunks(q), to_chunks(sel), to_chunks(sel_valid)
    starts = jnp.arange(n_chunks, dtype=jnp.int32) * Q_CHUNK
    key_offs = jnp.arange(MOBA_BLOCK, dtype=jnp.int32)
    n_sel = top_k * MOBA_BLOCK

    def attend_chunk(args):
        qc, selc, validc, t0 = args
        t_pos = t0 + jnp.arange(Q_CHUNK, dtype=jnp.int32)
        own = t0 // MOBA_BLOCK
        k_own = lax.dynamic_index_in_dim(k_blocks, own, axis=1, keepdims=False)
        v_own = lax.dynamic_index_in_dim(v_blocks, own, axis=1, keepdims=False)
        k_sel = jax.vmap(lambda kb, ix: kb[ix])(k_blocks, selc)
        v_sel = jax.vmap(lambda vb, ix: vb[ix])(v_blocks, selc)
        s_sel = jnp.einsum('nqd,nqjkd->nqjk', qc, k_sel).astype(jnp.float32) * scale
        dist_sel = (t_pos[None, :, None, None] - (selc[..., None] * MOBA_BLOCK + key_offs)).astype(jnp.float32)
        s_sel = jnp.where(validc[..., None], s_sel - slopes_n[:, None, None, None] * dist_sel, NEG_INF)
        own_pos = own * MOBA_BLOCK + key_offs
        s_own = jnp.einsum('nqd,nkd->nqk', qc, k_own).astype(jnp.float32) * scale
        dist_own = (t_pos[:, None] - own_pos[None, :]).astype(jnp.float32)
        s_own = jnp.where((dist_own >= 0)[None], s_own - slopes_n[:, None, None] * dist_own, NEG_INF)
        scores = jnp.concatenate([s_sel.reshape(BH, Q_CHUNK, n_sel), s_own], axis=-1)
        p = jax.nn.softmax(scores, axis=-1).astype(v.dtype)
        return (jnp.einsum('nqk,nqkd->nqd', p[..., :n_sel], v_sel.reshape(BH, Q_CHUNK, n_sel, dh))
                + jnp.einsum('nqk,nkd->nqd', p[..., n_sel:], v_own))

    out = lax.map(attend_chunk, (q_c, sel_c, valid_c, starts))
    out = out.swapaxes(0, 1).reshape(B, H, S, dh).transpose(0, 2, 1, 3)
    return out.reshape(B, S, H * dh)


def _ssm_combine(e1, e2):
    a1r, a1i, b1r, b1i = e1
    a2r, a2i, b2r, b2i = e2
    return (a2r * a1r - a2i * a1i,
            a2r * a1i + a2i * a1r,
            a2r * b1r - a2i * b1i + b2r,
            a2r * b1i + a2i * b1r + b2i)


def _s5_layer(u, lam_re, lam_im, log_dt, b_re, b_im, c_re, c_im, d_skip, w_glu, b_glu):
    B, S, _ = u.shape
    u32 = u.astype(jnp.float32).reshape(B, S, N_SSM_GROUPS, SSM_GROUP)
    lam_re = lam_re.astype(jnp.float32)
    lam_im = lam_im.astype(jnp.float32)
    dt = jnp.exp(log_dt.astype(jnp.float32))[:, None]
    decay = jnp.exp(lam_re * dt)
    ab_re = decay * jnp.cos(lam_im * dt)
    ab_im = decay * jnp.sin(lam_im * dt)
    den = lam_re * lam_re + lam_im * lam_im
    coef_re = ((ab_re - 1.0) * lam_re + ab_im * lam_im) / den
    coef_im = (ab_im * lam_re - (ab_re - 1.0) * lam_im) / den
    b_re = b_re.astype(jnp.float32)
    b_im = b_im.astype(jnp.float32)
    bb_re = coef_re[..., None] * b_re - coef_im[..., None] * b_im
    bb_im = coef_re[..., None] * b_im + coef_im[..., None] * b_re
    bu_re = jnp.einsum('bsgh,gph->sbgp', u32, bb_re)
    bu_im = jnp.einsum('bsgh,gph->sbgp', u32, bb_im)
    a_re = jnp.broadcast_to(ab_re[None, None], (S, 1, N_SSM_GROUPS, SSM_STATE))
    a_im = jnp.broadcast_to(ab_im[None, None], (S, 1, N_SSM_GROUPS, SSM_STATE))
    _, _, x_re, x_im = lax.associative_scan(_ssm_combine, (a_re, a_im, bu_re, bu_im), axis=0)
    y = (jnp.einsum('sbgp,ghp->bsgh', x_re, c_re.astype(jnp.float32))
         - jnp.einsum('sbgp,ghp->bsgh', x_im, c_im.astype(jnp.float32)))
    y = y.reshape(B, S, SSM_WIDTH) + d_skip.astype(jnp.float32) * u32.reshape(B, S, SSM_WIDTH)
    y = jax.nn.gelu(y)
    y = y * jax.nn.sigmoid(y @ w_glu.astype(jnp.float32) + b_glu.astype(jnp.float32))
    return y.astype(u.dtype)


def setup_inputs(seed: int = 0) -> dict:
    key = jax.random.key(seed)
    ks = jax.random.split(key, 32)
    f32 = jnp.float32
    L, D, G, P, Hg = DEPTH, D_MODEL, N_SSM_GROUPS, SSM_STATE, SSM_GROUP

    def nrm(k, shape, s):
        return jax.random.normal(k, shape, f32) * s

    return {
        'x': nrm(ks[0], (BATCH, SEQ, D), 1.0),
        'c': nrm(ks[1], (BATCH, D), 1.0),
        'w_ada': nrm(ks[2], (L, D, N_MOD * D), 0.1 * D ** -0.5),
        'b_ada': nrm(ks[3], (L, N_MOD * D), 0.01),
        'ffn1_w_gate': nrm(ks[4], (L, D, D_FF), D ** -0.5),
        'ffn1_w_up': nrm(ks[5], (L, D, D_FF), D ** -0.5),
        'ffn1_w_down': nrm(ks[6], (L, D_FF, D), BETA * D_FF ** -0.5),
        'ln1_g': 1.0 + nrm(ks[7], (L, D), 0.01),
        'ln1_b': nrm(ks[8], (L, D), 0.01),
        'w_in': nrm(ks[9], (L, D, IN_PROJ_WIDTH), D ** -0.5),
        'attn_norm_g': 1.0 + nrm(ks[10], (L, ATTN_WIDTH), 0.01),
        'ssm_lambda_re': -0.5 + nrm(ks[11], (L, G, P), 0.01),
        'ssm_lambda_im': math.pi * jnp.arange(P, dtype=f32) + nrm(ks[12], (L, G, P), 0.01),
        'ssm_log_dt': jax.random.uniform(ks[13], (L, G), f32, math.log(DT_MIN), math.log(DT_MAX)),
        'ssm_b_re': nrm(ks[14], (L, G, P, Hg), (2.0 * Hg) ** -0.5),
        'ssm_b_im': nrm(ks[15], (L, G, P, Hg), (2.0 * Hg) ** -0.5),
        'ssm_c_re': nrm(ks[16], (L, G, Hg, P), (2.0 * P) ** -0.5),
        'ssm_c_im': nrm(ks[17], (L, G, Hg, P), (2.0 * P) ** -0.5),
        'ssm_d': nrm(ks[18], (L, SSM_WIDTH), 1.0),
        'ssm_w_glu': nrm(ks[19], (L, SSM_WIDTH, SSM_WIDTH), SSM_WIDTH ** -0.5),
        'ssm_b_glu': nrm(ks[20], (L, SSM_WIDTH), 0.01),
        'ssm_norm_g': 1.0 + nrm(ks[21], (L, SSM_WIDTH), 0.01),
        'w_out': nrm(ks[22], (L, D_MIX, D), BETA * D_MIX ** -0.5),
        'ln2_g': 1.0 + nrm(ks[23], (L, D), 0.01),
        'ln2_b': nrm(ks[24], (L, D), 0.01),
        'ffn2_w_gate': nrm(ks[25], (L, D, D_FF), D ** -0.5),
        'ffn2_w_up': nrm(ks[26], (L, D, D_FF), D ** -0.5),
        'ffn2_w_down': nrm(ks[27], (L, D_FF, D), BETA * D_FF ** -0.5),
        'ln3_g': 1.0 + nrm(ks[28], (L, D), 0.01),
        'ln3_b': nrm(ks[29], (L, D), 0.01),
    }


def reference(x, c, w_ada, b_ada, ffn1_w_gate, ffn1_w_up, ffn1_w_down, ln1_g, ln1_b,
              w_in, attn_norm_g, ssm_lambda_re, ssm_lambda_im, ssm_log_dt, ssm_b_re, ssm_b_im,
              ssm_c_re, ssm_c_im, ssm_d, ssm_w_glu, ssm_b_glu, ssm_norm_g, w_out, ln2_g, ln2_b,
              ffn2_w_gate, ffn2_w_up, ffn2_w_down, ln3_g, ln3_b):
    B, S, D = x.shape
    c_act = jax.nn.silu(c)
    for l in range(DEPTH):
        mod = (c_act @ w_ada[l] + b_ada[l]).reshape(B, N_MOD, D)[:, :, None, :]

        y = _swiglu(_modulate(x, mod[:, 0], mod[:, 1]), ffn1_w_gate[l], ffn1_w_up[l], ffn1_w_down[l])
        x = _layer_norm(ALPHA * x + 0.5 * (1.0 + mod[:, 2]) * y, ln1_g[l], ln1_b[l])

        h = _modulate(x, mod[:, 3], mod[:, 4]) @ w_in[l]
        q = h[..., :ATTN_WIDTH].reshape(B, S, N_HEADS, HEAD_DIM)
        k = h[..., ATTN_WIDTH:2 * ATTN_WIDTH].reshape(B, S, N_HEADS, HEAD_DIM)
        v = h[..., 2 * ATTN_WIDTH:3 * ATTN_WIDTH].reshape(B, S, N_HEADS, HEAD_DIM)
        u = h[..., 3 * ATTN_WIDTH:]
        attn_out = _rms_norm(_moba_attention(q, k, v), attn_norm_g[l])
        ssm_out = _rms_norm(_s5_layer(u, ssm_lambda_re[l], ssm_lambda_im[l], ssm_log_dt[l],
                                      ssm_b_re[l], ssm_b_im[l], ssm_c_re[l], ssm_c_im[l],
                                      ssm_d[l], ssm_w_glu[l], ssm_b_glu[l]), ssm_norm_g[l])
        mix = jnp.concatenate([attn_out, ssm_out], axis=-1) @ w_out[l]
        x = _layer_norm(ALPHA * x + (1.0 + mod[:, 5]) * mix, ln2_g[l], ln2_b[l])

        y = _swiglu(_modulate(x, mod[:, 6], mod[:, 7]), ffn2_w_gate[l], ffn2_w_up[l], ffn2_w_down[l])
        x = _layer_norm(ALPHA * x + 0.5 * (1.0 + mod[:, 8]) * y, ln3_g[l], ln3_b[l])
    return x
```

```python
import functools
import math

import jax
import jax.numpy as jnp
from jax import lax
from jax.experimental import pallas as pl
from jax.experimental.pallas import tpu as pltpu

F32 = jnp.float32
BF16 = jnp.bfloat16

N_HEADS = 8
HEAD_DIM = 128
MOBA_BLOCK = 256
MOBA_TOPK = 3
SSM_GROUP = 16
SSM_STATE = 64
N_MOD = 9
LN_EPS = 1e-5
NEG_INF = -1e30
DEPTH = 1
ALPHA = (2.0 * DEPTH) ** 0.25

LANES = 128
SUBLANES = 8
VMEM_LIMIT_BYTES = 56 * 1024 * 1024

SLAB = LANES
HALF_STATES = 256


def _params(sem):
    return pltpu.CompilerParams(dimension_semantics=sem, vmem_limit_bytes=VMEM_LIMIT_BYTES)


def _layer_norm_rows(r, g, b):
    mu = jnp.mean(r, axis=-1, keepdims=True)
    d = r - mu
    var = jnp.mean(d * d, axis=-1, keepdims=True)
    return d * lax.rsqrt(var + LN_EPS) * g + b


def _rms_norm_rows(y, g):
    return y * lax.rsqrt(jnp.mean(y * y, axis=-1, keepdims=True) + LN_EPS) * g


def _adaln_kernel(c_ref, w_ref, b_ref, o_ref):
    c = c_ref[...]
    c_act = (c * jax.nn.sigmoid(c)).astype(BF16)
    o_ref[...] = jnp.dot(c_act, w_ref[...].astype(BF16), preferred_element_type=F32) + b_ref[...]


def _adaln(c_pad, w_ada, b_ada, tn=1024):
    rows, d = c_pad.shape
    n = w_ada.shape[1]
    return pl.pallas_call(
        _adaln_kernel,
        grid=(n // tn,),
        in_specs=[pl.BlockSpec((rows, d), lambda j: (0, 0)),
                  pl.BlockSpec((d, tn), lambda j: (0, j)),
                  pl.BlockSpec((1, tn), lambda j: (0, j))],
        out_specs=pl.BlockSpec((rows, tn), lambda j: (0, j)),
        out_shape=jax.ShapeDtypeStruct((rows, n), F32),
        compiler_params=_params(("arbitrary",)),
        name="adaln",
    )(c_pad, w_ada, b_ada.reshape(1, n))


def _ffn_kernel(x_ref, mod_ref, wg_ref, wu_ref, wd_ref, lng_ref, lnb_ref, o_ref, u_scr, acc_scr, *, mod_base):
    f = pl.program_id(1)

    @pl.when(f == 0)
    def _():
        shift = mod_ref[0, mod_base:mod_base + 1, :]
        scale = mod_ref[0, mod_base + 1:mod_base + 2, :]
        u_scr[...] = (x_ref[...] * (1.0 + scale) + shift).astype(BF16)
        acc_scr[...] = jnp.zeros_like(acc_scr)

    u = u_scr[...]
    g = jnp.dot(u, wg_ref[...], preferred_element_type=F32)
    up = jnp.dot(u, wu_ref[...], preferred_element_type=F32)
    h = (g * jax.nn.sigmoid(g)) * up
    acc_scr[...] += jnp.dot(h.astype(BF16), wd_ref[...], preferred_element_type=F32)

    @pl.when(f == pl.num_programs(1) - 1)
    def _():
        gate = mod_ref[0, mod_base + 2:mod_base + 3, :]
        r = ALPHA * x_ref[...] + (0.5 * (1.0 + gate)) * acc_scr[...]
        o_ref[...] = _layer_norm_rows(r, lng_ref[...], lnb_ref[...])


def _ffn(x2d, mod, wg, wu, wd, ln_g, ln_b, *, mod_base, seq, tm=512, tf=512):
    n, d = x2d.shape
    dff = wg.shape[1]
    tiles_per_batch = seq // tm
    return pl.pallas_call(
        functools.partial(_ffn_kernel, mod_base=mod_base),
        grid=(n // tm, dff // tf),
        in_specs=[pl.BlockSpec((tm, d), lambda m, f: (m, 0)),
                  pl.BlockSpec((1, N_MOD, d), lambda m, f: (m // tiles_per_batch, 0, 0)),
                  pl.BlockSpec((d, tf), lambda m, f: (0, f)),
                  pl.BlockSpec((d, tf), lambda m, f: (0, f)),
                  pl.BlockSpec((tf, d), lambda m, f: (f, 0)),
                  pl.BlockSpec((1, d), lambda m, f: (0, 0)),
                  pl.BlockSpec((1, d), lambda m, f: (0, 0))],
        out_specs=pl.BlockSpec((tm, d), lambda m, f: (m, 0)),
        out_shape=jax.ShapeDtypeStruct((n, d), F32),
        scratch_shapes=[pltpu.VMEM((tm, d), BF16), pltpu.VMEM((tm, d), F32)],
        compiler_params=_params(("parallel", "arbitrary")),
        name="ffn",
    )(x2d, mod, wg, wu, wd, ln_g.reshape(1, d), ln_b.reshape(1, d))


def _inproj_kernel(x_ref, mod_ref, w_ref, q_ref, k_ref, v_ref, u_ref, xm_scr):
    j = pl.program_id(1)

    @pl.when(j == 0)
    def _():
        shift = mod_ref[0, 3:4, :]
        scale = mod_ref[0, 4:5, :]
        xm_scr[...] = (x_ref[...] * (1.0 + scale) + shift).astype(BF16)

    h = jnp.dot(xm_scr[...], w_ref[...], preferred_element_type=F32)

    @pl.when(j == 0)
    def _():
        q_ref[...] = h.astype(BF16)

    @pl.when(j == 1)
    def _():
        k_ref[...] = h.astype(BF16)

    @pl.when(j == 2)
    def _():
        v_ref[...] = h.astype(BF16)

    @pl.when(j == 3)
    def _():
        u_ref[...] = h


def _inproj(x2d, mod, w_in, *, seq, tm=512):
    n, d = x2d.shape
    width = w_in.shape[1] // 4
    tiles_per_batch = seq // tm
    out_spec = pl.BlockSpec((tm, width), lambda m, j: (m, 0))
    return pl.pallas_call(
        _inproj_kernel,
        grid=(n // tm, 4),
        in_specs=[pl.BlockSpec((tm, d), lambda m, j: (m, 0)),
                  pl.BlockSpec((1, N_MOD, d), lambda m, j: (m // tiles_per_batch, 0, 0)),
                  pl.BlockSpec((d, width), lambda m, j: (0, j))],
        out_specs=[out_spec, out_spec, out_spec, out_spec],
        out_shape=[jax.ShapeDtypeStruct((n, width), BF16)] * 3 + [jax.ShapeDtypeStruct((n, width), F32)],
        scratch_shapes=[pltpu.VMEM((tm, d), BF16)],
        compiler_params=_params(("parallel", "arbitrary")),
        name="inproj",
    )(x2d, mod, w_in)


def _dot_nt(a, b):
    return lax.dot_general(a, b, (((1,), (1,)), ((), ())), preferred_element_type=F32)


def _attn_kernel(slopes_ref, q_ref, k_ref, v_ref, o_ref, kmean_scr, *, n_blocks):
    h = pl.program_id(1)
    i = pl.program_id(2)
    blk = MOBA_BLOCK
    scale = HEAD_DIM ** -0.5
    slope = slopes_ref[h]

    @pl.when(i == 0)
    def _():
        kmean_scr[...] = jnp.zeros_like(kmean_scr)
        for j in range(n_blocks):
            kb = k_ref[0, j * blk:(j + 1) * blk, :].astype(F32)
            kmean_scr[j:j + 1, :] = jnp.mean(kb, axis=0, keepdims=True)

    q = q_ref[0]

    gate = _dot_nt(q, kmean_scr[...].astype(BF16))
    col = lax.broadcasted_iota(jnp.int32, gate.shape, 1).astype(F32)
    i_f = i.astype(F32)
    work = jnp.where(col < i_f, gate, NEG_INF)
    picked = []
    for _ in range(MOBA_TOPK):
        mx = jnp.max(work, axis=1, keepdims=True)
        idx = jnp.min(jnp.where(work == mx, col, float(LANES)), axis=1, keepdims=True)
        work = jnp.where(col == idx, -jnp.inf, work)
        picked.append(jnp.where(idx < i_f, idx, -1.0))

    row_id = lax.broadcasted_iota(jnp.int32, (blk, blk), 0)
    col_id = lax.broadcasted_iota(jnp.int32, (blk, blk), 1)
    rel = (row_id - col_id).astype(F32)

    k_own = k_ref[0, pl.ds(pl.multiple_of(i * blk, blk), blk), :]
    v_own = v_ref[0, pl.ds(pl.multiple_of(i * blk, blk), blk), :]
    s = _dot_nt(q, k_own) * scale
    s = jnp.where(rel >= 0.0, s - slope * rel, NEG_INF)
    m0 = jnp.max(s, axis=1, keepdims=True)
    p = jnp.exp(s - m0)
    l0 = jnp.sum(p, axis=1, keepdims=True)
    acc0 = jnp.dot(p.astype(BF16), v_own, preferred_element_type=F32)

    def body(j, carry):
        m, l, acc = carry
        off = pl.multiple_of(j * blk, blk)
        kj = k_ref[0, pl.ds(off, blk), :]
        vj = v_ref[0, pl.ds(off, blk), :]
        j_f = j.astype(F32)
        dist = rel + (i_f - j_f) * float(blk)
        sj = _dot_nt(q, kj) * scale - slope * dist
        selected = (picked[0] == j_f) | (picked[1] == j_f) | (picked[2] == j_f)
        sj = jnp.where(selected, sj, NEG_INF)
        m_new = jnp.maximum(m, jnp.max(sj, axis=1, keepdims=True))
        a = jnp.exp(m - m_new)
        pj = jnp.exp(sj - m_new)
        l_new = a * l + jnp.sum(pj, axis=1, keepdims=True)
        acc_new = a * acc + jnp.dot(pj.astype(BF16), vj, preferred_element_type=F32)
        return m_new, l_new, acc_new

    m, l, acc = lax.fori_loop(0, i, body, (m0, l0, acc0))
    o_ref[0] = acc / l


def _attention(q, k, v, slopes):
    b, s, width = q.shape
    n_heads = width // HEAD_DIM
    n_blocks = s // MOBA_BLOCK
    kv_spec = pl.BlockSpec((1, s, HEAD_DIM), lambda bi, h, i: (bi, 0, h))
    qo_spec = pl.BlockSpec((1, MOBA_BLOCK, HEAD_DIM), lambda bi, h, i: (bi, i, h))
    return pl.pallas_call(
        functools.partial(_attn_kernel, n_blocks=n_blocks),
        grid=(b, n_heads, n_blocks),
        in_specs=[pl.BlockSpec(memory_space=pltpu.SMEM), qo_spec, kv_spec, kv_spec],
        out_specs=qo_spec,
        out_shape=jax.ShapeDtypeStruct((b, s, width), F32),
        scratch_shapes=[pltpu.VMEM((LANES, HEAD_DIM), F32)],
        compiler_params=_params(("parallel", "parallel", "arbitrary")),
        name="moba_attn",
    )(slopes, q, k, v)


def _ssm_tables(lam_re, lam_im, log_dt, b_re, b_im, c_re, c_im):
    g_total, p = lam_re.shape
    n_slab = g_total * SSM_GROUP // SLAB
    gps = SLAB // SSM_GROUP
    gph = gps // 2
    dt = jnp.exp(log_dt)[:, None]
    decay = jnp.exp(lam_re * dt)
    ab_re = decay * jnp.cos(lam_im * dt)
    ab_im = decay * jnp.sin(lam_im * dt)
    den = lam_re * lam_re + lam_im * lam_im
    coef_re = ((ab_re - 1.0) * lam_re + ab_im * lam_im) / den
    coef_im = (ab_im * lam_re - (ab_re - 1.0) * lam_im) / den
    bb_re = coef_re[..., None] * b_re - coef_im[..., None] * b_im
    bb_im = coef_re[..., None] * b_im + coef_im[..., None] * b_re
    eye = jnp.eye(gps, dtype=F32)

    bbri = jnp.stack([bb_re, bb_im], axis=0).reshape(2, n_slab, gps, p, SSM_GROUP)
    bb = jnp.einsum('rsgph,gk->sghkrp', bbri, eye)
    bb = bb.reshape(n_slab, gps, SSM_GROUP, 2, gph, 2, p).transpose(0, 1, 2, 3, 5, 4, 6)
    bb = bb.reshape(n_slab, SLAB, 4 * HALF_STATES)

    ccri = jnp.stack([c_re, -c_im], axis=0).reshape(2, n_slab, gps, SSM_GROUP, p)
    cc = jnp.einsum('rsghp,gk->skrpgh', ccri, eye)
    cc = cc.reshape(n_slab, 2, gph, 2, p, gps, SSM_GROUP).transpose(0, 1, 3, 2, 4, 5, 6)
    cc = cc.reshape(n_slab, 2, 2 * HALF_STATES, SLAB)

    def rows(a):
        a = a.reshape(n_slab, 2, 1, HALF_STATES)
        return jnp.broadcast_to(a, (n_slab, 2, 4, HALF_STATES)).reshape(n_slab, SUBLANES, HALF_STATES)

    return bb.astype(BF16), cc.astype(BF16), rows(ab_re), rows(ab_im)


def _ssm_kernel(u_ref, bb_ref, cc_ref, are_ref, aim_ref, d_ref, y_ref, x_scr, carry_scr, *, t_chunk):
    c = pl.program_id(0)
    s = pl.program_id(1)
    rows = 4 * t_chunk
    hs = HALF_STATES
    n_cg = 2 * hs // LANES
    n_re = hs // LANES

    u = u_ref[...].reshape(rows, SLAB)
    ub = u.astype(BF16)
    for half in range(2):
        bu = jnp.dot(ub, bb_ref[0, :, half * 2 * hs:(half + 1) * 2 * hs], preferred_element_type=F32)
        for g in range(n_cg):
            x_scr[g, half * rows:(half + 1) * rows, :] = bu[:, g * LANES:(g + 1) * LANES]

    @pl.when(c == 0)
    def _():
        carry_scr[s] = jnp.zeros((n_cg, SUBLANES, LANES), F32)

    a_re = [are_ref[0, :, g * LANES:(g + 1) * LANES] for g in range(n_re)]
    a_im = [aim_ref[0, :, g * LANES:(g + 1) * LANES] for g in range(n_re)]
    x0 = tuple(carry_scr[s, g] for g in range(n_cg))

    def step(t, carry):
        idx = pl.ds(t, SUBLANES, stride=t_chunk)
        new = [None] * n_cg
        for g in range(n_re):
            x_re, x_im = carry[g], carry[n_re + g]
            new[g] = a_re[g] * x_re - a_im[g] * x_im + x_scr[g, idx, :]
            new[n_re + g] = a_re[g] * x_im + a_im[g] * x_re + x_scr[n_re + g, idx, :]
        for g in range(n_cg):
            x_scr[g, idx, :] = new[g]
        return tuple(new)

    x_fin = lax.fori_loop(0, t_chunk, step, x0, unroll=8)
    for g in range(n_cg):
        carry_scr[s, g] = x_fin[g]

    y = None
    for half in range(2):
        xh = jnp.concatenate([x_scr[g, half * rows:(half + 1) * rows, :] for g in range(n_cg)], axis=1)
        yh = jnp.dot(xh.astype(BF16), cc_ref[0, half], preferred_element_type=F32)
        y = yh if y is None else y + yh
    y = y + d_ref[...] * u
    y_ref[...] = jax.nn.gelu(y).reshape(4, t_chunk, SLAB)


def _ssm(u, bb, cc, a_re, a_im, d_skip, t_chunk=256):
    b, s, width = u.shape
    assert b == 4
    n_slab = width // SLAB
    hs = HALF_STATES
    return pl.pallas_call(
        functools.partial(_ssm_kernel, t_chunk=t_chunk),
        grid=(s // t_chunk, n_slab),
        in_specs=[pl.BlockSpec((b, t_chunk, SLAB), lambda c, sl: (0, c, sl)),
                  pl.BlockSpec((1, SLAB, 4 * hs), lambda c, sl: (sl, 0, 0)),
                  pl.BlockSpec((1, 2, 2 * hs, SLAB), lambda c, sl: (sl, 0, 0, 0)),
                  pl.BlockSpec((1, SUBLANES, hs), lambda c, sl: (sl, 0, 0)),
                  pl.BlockSpec((1, SUBLANES, hs), lambda c, sl: (sl, 0, 0)),
                  pl.BlockSpec((1, SLAB), lambda c, sl: (0, sl))],
        out_specs=pl.BlockSpec((b, t_chunk, SLAB), lambda c, sl: (0, c, sl)),
        out_shape=jax.ShapeDtypeStruct((b, s, width), F32),
        scratch_shapes=[pltpu.VMEM((2 * hs // LANES, 2 * b * t_chunk, LANES), F32),
                        pltpu.VMEM((n_slab, 2 * hs // LANES, SUBLANES, LANES), F32)],
        compiler_params=_params(("arbitrary", "arbitrary")),
        name="s5_scan",
    )(u, bb, cc, a_re, a_im, d_skip.reshape(1, width))


def _mix_kernel(attn_ref, y_ref, x_ref, mod_ref, wglu_ref, bglu_ref, ga_ref, gs_ref, wo_ref, lng_ref, lnb_ref, o_ref):
    aw = attn_ref.shape[1]
    a = _rms_norm_rows(attn_ref[...], ga_ref[...])
    y = y_ref[...]
    z = y * jax.nn.sigmoid(jnp.dot(y.astype(BF16), wglu_ref[...], preferred_element_type=F32) + bglu_ref[...])
    sn = _rms_norm_rows(z, gs_ref[...])
    mix = (jnp.dot(a.astype(BF16), wo_ref[0:aw, :], preferred_element_type=F32)
           + jnp.dot(sn.astype(BF16), wo_ref[aw:, :], preferred_element_type=F32))
    gate = mod_ref[0, 5:6, :]
    r = ALPHA * x_ref[...] + (1.0 + gate) * mix
    o_ref[...] = _layer_norm_rows(r, lng_ref[...], lnb_ref[...])


def _mix(attn2d, y2d, x2d, mod, w_glu, b_glu, g_attn, g_ssm, w_out, ln_g, ln_b, *, seq, tm=512):
    n, d = x2d.shape
    aw = attn2d.shape[1]
    sw = y2d.shape[1]
    tiles_per_batch = seq // tm
    const = lambda m: (0, 0)
    return pl.pallas_call(
        _mix_kernel,
        grid=(n // tm,),
        in_specs=[pl.BlockSpec((tm, aw), lambda m: (m, 0)),
                  pl.BlockSpec((tm, sw), lambda m: (m, 0)),
                  pl.BlockSpec((tm, d), lambda m: (m, 0)),
                  pl.BlockSpec((1, N_MOD, d), lambda m: (m // tiles_per_batch, 0, 0)),
                  pl.BlockSpec((sw, sw), const),
                  pl.BlockSpec((1, sw), const),
                  pl.BlockSpec((1, aw), const),
                  pl.BlockSpec((1, sw), const),
                  pl.BlockSpec((aw + sw, d), const),
                  pl.BlockSpec((1, d), const),
                  pl.BlockSpec((1, d), const)],
        out_specs=pl.BlockSpec((tm, d), lambda m: (m, 0)),
        out_shape=jax.ShapeDtypeStruct((n, d), F32),
        compiler_params=_params(("parallel",)),
        name="mix",
    )(attn2d, y2d, x2d, mod, w_glu, b_glu.reshape(1, sw), g_attn.reshape(1, aw), g_ssm.reshape(1, sw),
      w_out, ln_g.reshape(1, d), ln_b.reshape(1, d))


def kernel(x, c, w_ada, b_ada, ffn1_w_gate, ffn1_w_up, ffn1_w_down, ln1_g, ln1_b, w_in, attn_norm_g, ssm_lambda_re, ssm_lambda_im, ssm_log_dt, ssm_b_re, ssm_b_im, ssm_c_re, ssm_c_im, ssm_d, ssm_w_glu, ssm_b_glu, ssm_norm_g, w_out, ln2_g, ln2_b, ffn2_w_gate, ffn2_w_up, ffn2_w_down, ln3_g, ln3_b):
    b, s, d = x.shape
    assert w_ada.shape[0] == DEPTH
    slopes = 2.0 ** (-(8.0 / N_HEADS) * jnp.arange(1, N_HEADS + 1, dtype=F32))
    c_pad = jnp.pad(c, ((0, SUBLANES - b), (0, 0)))
    x2d = x.reshape(b * s, d)
    for l in range(DEPTH):
        mod = _adaln(c_pad, w_ada[l], b_ada[l])[:b].reshape(b, N_MOD, d)

        x2d = _ffn(x2d, mod, ffn1_w_gate[l].astype(BF16), ffn1_w_up[l].astype(BF16), ffn1_w_down[l].astype(BF16),
                   ln1_g[l], ln1_b[l], mod_base=0, seq=s)

        q, k, v, u = _inproj(x2d, mod, w_in[l].astype(BF16), seq=s)
        aw = q.shape[1]
        attn = _attention(q.reshape(b, s, aw), k.reshape(b, s, aw), v.reshape(b, s, aw), slopes)

        bb, cc, a_re, a_im = _ssm_tables(ssm_lambda_re[l], ssm_lambda_im[l], ssm_log_dt[l],
                                         ssm_b_re[l], ssm_b_im[l], ssm_c_re[l], ssm_c_im[l])
        sw = u.shape[1]
        y = _ssm(u.reshape(b, s, sw), bb, cc, a_re, a_im, ssm_d[l])

        x2d = _mix(attn.reshape(b * s, aw), y.reshape(b * s, sw), x2d, mod, ssm_w_glu[l].astype(BF16), ssm_b_glu[l],
                   attn_norm_g[l], ssm_norm_g[l], w_out[l].astype(BF16), ln2_g[l], ln2_b[l], seq=s)

        x2d = _ffn(x2d, mod, ffn2_w_gate[l].astype(BF16), ffn2_w_up[l].astype(BF16), ffn2_w_down[l].astype(BF16),
                   ln3_g[l], ln3_b[l], mod_base=6, seq=s)
    return x2d.reshape(b, s, d)
```

```python
import functools
import math

import jax
import jax.numpy as jnp
from jax import lax
from jax.experimental import pallas as pl
from jax.experimental.pallas import tpu as pltpu

F32 = jnp.float32
BF16 = jnp.bfloat16

N_HEADS = 8
HEAD_DIM = 128
MOBA_BLOCK = 256
MOBA_TOPK = 3
SSM_GROUP = 16
SSM_STATE = 64
N_MOD = 9
LN_EPS = 1e-5
NEG_INF = -1e30
LOG2E = math.log2(math.e)
DEPTH = 1
ALPHA = (2.0 * DEPTH) ** 0.25

LANES = 128
SUBLANES = 8
VMEM_LIMIT_BYTES = 56 * 1024 * 1024

SLAB = LANES
HALF_STATES = 256


def _params(sem):
    return pltpu.CompilerParams(dimension_semantics=sem, vmem_limit_bytes=VMEM_LIMIT_BYTES)


def _layer_norm_rows(r, g, b):
    mu = jnp.mean(r, axis=-1, keepdims=True)
    d = r - mu
    var = jnp.mean(d * d, axis=-1, keepdims=True)
    return d * lax.rsqrt(var + LN_EPS) * g + b


def _rms_norm_rows(y, g):
    return y * lax.rsqrt(jnp.mean(y * y, axis=-1, keepdims=True) + LN_EPS) * g


def _adaln_kernel(c_ref, w_ref, b_ref, o_ref):
    c = c_ref[...]
    c_act = (c * jax.nn.sigmoid(c)).astype(BF16)
    o_ref[...] = jnp.dot(c_act, w_ref[...].astype(BF16), preferred_element_type=F32) + b_ref[...]


def _adaln(c_pad, w_ada, b_ada, tn=1024):
    rows, d = c_pad.shape
    n = w_ada.shape[1]
    return pl.pallas_call(
        _adaln_kernel,
        grid=(n // tn,),
        in_specs=[pl.BlockSpec((rows, d), lambda j: (0, 0)),
                  pl.BlockSpec((d, tn), lambda j: (0, j)),
                  pl.BlockSpec((1, tn), lambda j: (0, j))],
        out_specs=pl.BlockSpec((rows, tn), lambda j: (0, j)),
        out_shape=jax.ShapeDtypeStruct((rows, n), F32),
        compiler_params=_params(("arbitrary",)),
        name="adaln",
    )(c_pad, w_ada, b_ada.reshape(1, n))


def _ffn_kernel(x_ref, mod_ref, wg_ref, wu_ref, wd_ref, lng_ref, lnb_ref, o_ref, u_scr, acc_scr, *, mod_base):
    f = pl.program_id(1)

    @pl.when(f == 0)
    def _():
        shift = mod_ref[0, mod_base:mod_base + 1, :]
        scale = mod_ref[0, mod_base + 1:mod_base + 2, :]
        u_scr[...] = (x_ref[...] * (1.0 + scale) + shift).astype(BF16)
        acc_scr[...] = jnp.zeros_like(acc_scr)

    u = u_scr[...]
    g = jnp.dot(u, wg_ref[...], preferred_element_type=F32)
    up = jnp.dot(u, wu_ref[...], preferred_element_type=F32)
    h = (g * jax.nn.sigmoid(g)) * up
    acc_scr[...] += jnp.dot(h.astype(BF16), wd_ref[...], preferred_element_type=F32)

    @pl.when(f == pl.num_programs(1) - 1)
    def _():
        gate = mod_ref[0, mod_base + 2:mod_base + 3, :]
        r = ALPHA * x_ref[...] + (0.5 * (1.0 + gate)) * acc_scr[...]
        o_ref[...] = _layer_norm_rows(r, lng_ref[...], lnb_ref[...])


def _ffn(x2d, mod, wg, wu, wd, ln_g, ln_b, *, mod_base, seq, tm=512, tf=512):
    n, d = x2d.shape
    dff = wg.shape[1]
    tiles_per_batch = seq // tm
    return pl.pallas_call(
        functools.partial(_ffn_kernel, mod_base=mod_base),
        grid=(n // tm, dff // tf),
        in_specs=[pl.BlockSpec((tm, d), lambda m, f: (m, 0)),
                  pl.BlockSpec((1, N_MOD, d), lambda m, f: (m // tiles_per_batch, 0, 0)),
                  pl.BlockSpec((d, tf), lambda m, f: (0, f)),
                  pl.BlockSpec((d, tf), lambda m, f: (0, f)),
                  pl.BlockSpec((tf, d), lambda m, f: (f, 0)),
                  pl.BlockSpec((1, d), lambda m, f: (0, 0)),
                  pl.BlockSpec((1, d), lambda m, f: (0, 0))],
        out_specs=pl.BlockSpec((tm, d), lambda m, f: (m, 0)),
        out_shape=jax.ShapeDtypeStruct((n, d), F32),
        scratch_shapes=[pltpu.VMEM((tm, d), BF16), pltpu.VMEM((tm, d), F32)],
        compiler_params=_params(("parallel", "arbitrary")),
        name="ffn",
    )(x2d, mod, wg, wu, wd, ln_g.reshape(1, d), ln_b.reshape(1, d))


def _inproj_kernel(x_ref, mod_ref, w_ref, u_init_ref, q_ref, k_ref, v_ref, u_ref, xm_scr):
    del u_init_ref
    j = pl.program_id(1)

    @pl.when(j == 0)
    def _():
        shift = mod_ref[0, 3:4, :]
        scale = mod_ref[0, 4:5, :]
        xm_scr[...] = (x_ref[...] * (1.0 + scale) + shift).astype(BF16)

    h = jnp.dot(xm_scr[...], w_ref[...], preferred_element_type=F32)

    @pl.when(j == 0)
    def _():
        q_ref[...] = h.astype(BF16)

    @pl.when(j == 1)
    def _():
        k_ref[...] = h.astype(BF16)

    @pl.when(j == 2)
    def _():
        v_ref[...] = h.astype(BF16)

    @pl.when(j == 3)
    def _():
        u_ref[...] = h


def _inproj(x2d, mod, w_in, *, seq, tm=512):
    n, d = x2d.shape
    width = w_in.shape[1] // 4
    tiles_per_batch = seq // tm
    assert n // seq <= SUBLANES // 2
    out_spec = pl.BlockSpec((tm, width), lambda m, j: (m, 0))
    u_spec = pl.BlockSpec((tm, width), lambda m, j: (m % tiles_per_batch, m // tiles_per_batch))
    u_zeros = jnp.zeros((seq, SUBLANES * width), F32)
    return pl.pallas_call(
        _inproj_kernel,
        grid=(n // tm, 4),
        in_specs=[pl.BlockSpec((tm, d), lambda m, j: (m, 0)),
                  pl.BlockSpec((1, N_MOD, d), lambda m, j: (m // tiles_per_batch, 0, 0)),
                  pl.BlockSpec((d, width), lambda m, j: (0, j)),
                  pl.BlockSpec(memory_space=pl.ANY)],
        out_specs=[out_spec, out_spec, out_spec, u_spec],
        out_shape=[jax.ShapeDtypeStruct((n, width), BF16)] * 3 + [jax.ShapeDtypeStruct((seq, SUBLANES * width), F32)],
        input_output_aliases={3: 3},
        scratch_shapes=[pltpu.VMEM((tm, d), BF16)],
        compiler_params=_params(("parallel", "arbitrary")),
        name="inproj",
    )(x2d, mod, w_in, u_zeros)


def _dot_nt(a, b):
    return lax.dot_general(a, b, (((1,), (1,)), ((), ())), preferred_element_type=F32)


def _attn_kernel(slopes_ref, q_ref, k_ref, v_ref, o_ref, kmean_scr, vt_scr, base_scr, acc_scr,
                 sa_scr, sb_scr, pa_scr, pb_scr, *, n_blocks, hg):
    g = pl.program_id(1)
    i = pl.program_id(2)
    blk = MOBA_BLOCK
    dh = HEAD_DIM
    pair = 2 * blk
    scale2 = dh ** -0.5 * LOG2E
    i_f = i.astype(F32)
    slope = [slopes_ref[g * hg + h] for h in range(hg)]
    hcols = [slice(h * dh, (h + 1) * dh) for h in range(hg)]

    key_id = lax.broadcasted_iota(jnp.int32, (blk, blk), 0)
    qry_id = lax.broadcasted_iota(jnp.int32, (blk, blk), 1)
    rel = (qry_id - key_id).astype(F32)

    @pl.when(i == 0)
    def _():
        for h in range(hg):
            base_scr[h] = (-LOG2E * slope[h]) * rel
            for j in range(n_blocks):
                rows = slice(j * blk, (j + 1) * blk)
                kmean_scr[h, j:j + 1, :] = jnp.mean(k_ref[0, rows, hcols[h]].astype(F32), axis=0, keepdims=True)
                vt_scr[h, :, rows] = v_ref[0, rows, hcols[h]].astype(F32).T.astype(BF16)

    own = pl.ds(pl.multiple_of(i * blk, blk), blk)
    q = [q_ref[0, :, hcols[h]] for h in range(hg)]
    n_pairs = n_blocks // 2

    def pair_keys(t):
        return pl.ds(pl.multiple_of(jnp.clip(t, 0, n_pairs - 1) * pair, pair), pair)

    def scores_into(t, s_scr):
        keys = pair_keys(t)
        for h in range(hg):
            s_scr[h] = _dot_nt(k_ref[0, keys, hcols[h]], q[h])

    def softmax_stage(t, s_scr, p_scr, ms, ls):
        new_m, new_l, alphas = [], [], []
        for h in range(hg):
            pk = picked[h]
            raw, shift_rows, maxes = [], [], []
            for u in range(2):
                j_f = (2 * t + u).astype(F32)
                selected = (pk[0] == j_f) | (pk[1] == j_f) | (pk[2] == j_f)
                row_bias = jnp.where(selected, (j_f - i_f) * (float(blk) * LOG2E * slope[h]), NEG_INF)
                x = s_scr[h, u * blk:(u + 1) * blk, :] * scale2 + base_scr[h]
                raw.append(x)
                shift_rows.append(row_bias)
                maxes.append(jnp.max(x, axis=0, keepdims=True) + row_bias)
            m_new = jnp.maximum(ms[h], jnp.maximum(maxes[0], maxes[1]))
            a = jnp.exp2(ms[h] - m_new)
            l_new = a * ls[h]
            for u in range(2):
                p = jnp.exp2(raw[u] - (m_new - shift_rows[u]))
                l_new = l_new + jnp.sum(p, axis=0, keepdims=True)
                p_scr[h, u * blk:(u + 1) * blk, :] = p.astype(BF16)
            new_m.append(m_new)
            new_l.append(l_new)
            alphas.append(a)
        return tuple(new_m), tuple(new_l), tuple(alphas)

    def accumulate(t, p_scr, alphas):
        keys = pair_keys(t)
        pv = [jnp.dot(vt_scr[h, :, keys], p_scr[h], preferred_element_type=F32) for h in range(hg)]
        for h in range(hg):
            acc_scr[h] = alphas[h] * acc_scr[h] + pv[h]

    scores_into(0, sa_scr)
    gates = [_dot_nt(kmean_scr[h].astype(BF16), q[h]) for h in range(hg)]
    s_own = [_dot_nt(k_ref[0, own, hcols[h]], q[h]) for h in range(hg)]
    picked, m0, l0, p_own = [], [], [], []
    for h in range(hg):
        blk_id = lax.broadcasted_iota(jnp.int32, gates[h].shape, 0).astype(F32)
        work = jnp.where(blk_id < i_f, gates[h], NEG_INF)
        pk = []
        for _ in range(MOBA_TOPK):
            mx = jnp.max(work, axis=0, keepdims=True)
            idx = jnp.min(jnp.where(work == mx, blk_id, float(n_blocks)), axis=0, keepdims=True)
            work = jnp.where(blk_id == idx, -jnp.inf, work)
            pk.append(jnp.where(idx < i_f, idx, -1.0))
        picked.append(pk)

        s = jnp.where(rel >= 0.0, s_own[h] * scale2 + base_scr[h], NEG_INF)
        m = jnp.max(s, axis=0, keepdims=True)
        p = jnp.exp2(s - m)
        m0.append(m)
        l0.append(jnp.sum(p, axis=0, keepdims=True))
        p_own.append(p.astype(BF16))
    pv_own = [jnp.dot(vt_scr[h, :, own], p_own[h], preferred_element_type=F32) for h in range(hg)]
    for h in range(hg):
        acc_scr[h] = pv_own[h]
    pb_scr[...] = jnp.zeros_like(pb_scr)

    def body(k, carry):
        ms, ls, a_prev = carry
        t0 = 2 * k
        scores_into(t0 + 1, sb_scr)
        ms, ls, a0 = softmax_stage(t0, sa_scr, pa_scr, ms, ls)
        accumulate(t0 - 1, pb_scr, a_prev)
        scores_into(t0 + 2, sa_scr)
        ms, ls, a1 = softmax_stage(t0 + 1, sb_scr, pb_scr, ms, ls)
        accumulate(t0, pa_scr, a0)
        return ms, ls, a1

    n_trips = (i + 3) // 4
    one_a = tuple(jnp.ones((1, blk), F32) for _ in range(hg))
    ms, ls, a_last = lax.fori_loop(0, n_trips, body, (tuple(m0), tuple(l0), one_a))
    accumulate(2 * n_trips - 1, pb_scr, a_last)
    for h in range(hg):
        o_ref[0, :, hcols[h]] = (acc_scr[h] / ls[h]).T


def _attention(q, k, v, slopes, hg=4):
    b, s, width = q.shape
    n_heads = width // HEAD_DIM
    n_blocks = s // MOBA_BLOCK
    assert n_blocks % 2 == 0 and n_heads % hg == 0
    kv_spec = pl.BlockSpec((1, s, hg * HEAD_DIM), lambda bi, g, i: (bi, 0, g))
    qo_spec = pl.BlockSpec((1, MOBA_BLOCK, hg * HEAD_DIM), lambda bi, g, i: (bi, i, g))
    return pl.pallas_call(
        functools.partial(_attn_kernel, n_blocks=n_blocks, hg=hg),
        grid=(b, n_heads // hg, n_blocks),
        in_specs=[pl.BlockSpec(memory_space=pltpu.SMEM), qo_spec, kv_spec, kv_spec],
        out_specs=qo_spec,
        out_shape=jax.ShapeDtypeStruct((b, s, width), F32),
        scratch_shapes=[pltpu.VMEM((hg, n_blocks, HEAD_DIM), F32),
                        pltpu.VMEM((hg, HEAD_DIM, s), BF16),
                        pltpu.VMEM((hg, MOBA_BLOCK, MOBA_BLOCK), F32),
                        pltpu.VMEM((hg, HEAD_DIM, MOBA_BLOCK), F32),
                        pltpu.VMEM((hg, 2 * MOBA_BLOCK, MOBA_BLOCK), F32),
                        pltpu.VMEM((hg, 2 * MOBA_BLOCK, MOBA_BLOCK), F32),
                        pltpu.VMEM((hg, 2 * MOBA_BLOCK, MOBA_BLOCK), BF16),
                        pltpu.VMEM((hg, 2 * MOBA_BLOCK, MOBA_BLOCK), BF16)],
        compiler_params=_params(("parallel", "parallel", "arbitrary")),
        name="moba_attn",
    )(slopes, q, k, v)


def _ssm_tables(lam_re, lam_im, log_dt, b_re, b_im, c_re, c_im):
    g_total, p = lam_re.shape
    n_slab = g_total * SSM_GROUP // SLAB
    gps = SLAB // SSM_GROUP
    gph = gps // 2
    dt = jnp.exp(log_dt)[:, None]
    decay = jnp.exp(lam_re * dt)
    ab_re = decay * jnp.cos(lam_im * dt)
    ab_im = decay * jnp.sin(lam_im * dt)
    den = lam_re * lam_re + lam_im * lam_im
    coef_re = ((ab_re - 1.0) * lam_re + ab_im * lam_im) / den
    coef_im = (ab_im * lam_re - (ab_re - 1.0) * lam_im) / den
    bb_re = coef_re[..., None] * b_re - coef_im[..., None] * b_im
    bb_im = coef_re[..., None] * b_im + coef_im[..., None] * b_re
    eye = jnp.eye(gps, dtype=F32)

    bbri = jnp.stack([bb_re, bb_im], axis=0).reshape(2, n_slab, gps, p, SSM_GROUP)
    bb = jnp.einsum('rsgph,gk->sghkrp', bbri, eye)
    bb = bb.reshape(n_slab, gps, SSM_GROUP, 2, gph, 2, p).transpose(0, 1, 2, 3, 5, 4, 6)
    bb = bb.reshape(n_slab, SLAB, 2, 2 * HALF_STATES).transpose(0, 2, 1, 3).reshape(n_slab, 2 * SLAB, 2 * HALF_STATES)

    ccri = jnp.stack([c_re, -c_im], axis=0).reshape(2, n_slab, gps, SSM_GROUP, p)
    cc = jnp.einsum('rsghp,gk->skrpgh', ccri, eye)
    cc = cc.reshape(n_slab, 2, gph, 2, p, gps, SSM_GROUP).transpose(0, 1, 3, 2, 4, 5, 6)
    cc = cc.reshape(n_slab, 2, 2 * HALF_STATES, SLAB).transpose(0, 2, 1, 3).reshape(n_slab, 2 * HALF_STATES, 2 * SLAB)

    def rows(a):
        a = a.reshape(n_slab, 2, 1, HALF_STATES)
        return jnp.broadcast_to(a, (n_slab, 2, 4, HALF_STATES)).reshape(n_slab, SUBLANES, HALF_STATES)

    return bb.astype(BF16), cc.astype(BF16), rows(ab_re), rows(ab_im)


def _ssm_kernel(u_ref, bb_ref, cc_ref, are_ref, aim_ref, d_ref, y_ref, x_scr, carry_scr, *, t_chunk):
    c = pl.program_id(0)
    s = pl.program_id(1)
    rows = SUBLANES * t_chunk
    hs = HALF_STATES
    n_cg = 2 * hs // LANES
    n_re = hs // LANES
    half_rows = SUBLANES // 2

    u3 = u_ref[...]
    u_lo = u3.reshape(rows, SLAB)
    u_hi = pltpu.roll(u3, half_rows, 1).reshape(rows, SLAB)
    lhs = jnp.concatenate([u_lo, u_hi], axis=1).astype(BF16)
    row_parts = [slice(k * rows // 2, (k + 1) * rows // 2) for k in range(2)]
    for part in row_parts:
        bu = jnp.dot(lhs[part], bb_ref[0], preferred_element_type=F32)
        for g in range(n_cg):
            x_scr[g, part, :] = bu[:, g * LANES:(g + 1) * LANES]

    @pl.when(c == 0)
    def _():
        carry_scr[s] = jnp.zeros((n_cg, SUBLANES, LANES), F32)

    a_re = [are_ref[0, :, g * LANES:(g + 1) * LANES] for g in range(n_re)]
    a_im = [aim_ref[0, :, g * LANES:(g + 1) * LANES] for g in range(n_re)]
    x0 = tuple(carry_scr[s, g] for g in range(n_cg))

    def step(t, carry):
        idx = pl.ds(pl.multiple_of(t * SUBLANES, SUBLANES), SUBLANES)
        new = [None] * n_cg
        for g in range(n_re):
            x_re, x_im = carry[g], carry[n_re + g]
            new[g] = a_re[g] * x_re - a_im[g] * x_im + x_scr[g, idx, :]
            new[n_re + g] = a_re[g] * x_im + a_im[g] * x_re + x_scr[n_re + g, idx, :]
        for g in range(n_cg):
            x_scr[g, idx, :] = new[g]
        return tuple(new)

    x_fin = lax.fori_loop(0, t_chunk, step, x0, unroll=8)
    for g in range(n_cg):
        carry_scr[s, g] = x_fin[g]

    yy = jnp.concatenate(
        [jnp.dot(jnp.concatenate([x_scr[g, part, :] for g in range(n_cg)], axis=1).astype(BF16), cc_ref[0],
                 preferred_element_type=F32) for part in row_parts], axis=0)
    y_half1 = pltpu.roll(yy[:, SLAB:].reshape(t_chunk, SUBLANES, SLAB), half_rows, 1)
    y3 = yy[:, :SLAB].reshape(t_chunk, SUBLANES, SLAB) + y_half1
    batch_rows = lax.broadcasted_iota(jnp.int32, y3.shape, 1) < half_rows
    y_ref[...] = jnp.where(batch_rows, y3, 0.0) + d_ref[...] * u3


def _ssm(u, bb, cc, a_re, a_im, d_skip, t_chunk=256):
    s, r, width = u.shape
    assert r == SUBLANES
    n_slab = width // SLAB
    hs = HALF_STATES
    return pl.pallas_call(
        functools.partial(_ssm_kernel, t_chunk=t_chunk),
        grid=(s // t_chunk, n_slab),
        in_specs=[pl.BlockSpec((t_chunk, SUBLANES, SLAB), lambda c, sl: (c, 0, sl)),
                  pl.BlockSpec((1, 2 * SLAB, 2 * hs), lambda c, sl: (sl, 0, 0)),
                  pl.BlockSpec((1, 2 * hs, 2 * SLAB), lambda c, sl: (sl, 0, 0)),
                  pl.BlockSpec((1, SUBLANES, hs), lambda c, sl: (sl, 0, 0)),
                  pl.BlockSpec((1, SUBLANES, hs), lambda c, sl: (sl, 0, 0)),
                  pl.BlockSpec((1, SLAB), lambda c, sl: (0, sl))],
        out_specs=pl.BlockSpec((t_chunk, SUBLANES, SLAB), lambda c, sl: (c, 0, sl)),
        out_shape=jax.ShapeDtypeStruct((s, SUBLANES, width), F32),
        scratch_shapes=[pltpu.VMEM((2 * hs // LANES, SUBLANES * t_chunk, LANES), F32),
                        pltpu.VMEM((n_slab, 2 * hs // LANES, SUBLANES, LANES), F32)],
        compiler_params=_params(("arbitrary", "arbitrary")),
        name="s5_scan",
    )(u, bb, cc, a_re, a_im, d_skip.reshape(1, width))


def _mix_kernel(attn_ref, y_ref, x_ref, mod_ref, wglu_ref, bglu_ref, ga_ref, gs_ref, wo_ref, lng_ref, lnb_ref, o_ref):
    aw = attn_ref.shape[1]
    a = _rms_norm_rows(attn_ref[...], ga_ref[...])
    y = jax.nn.gelu(y_ref[...])
    z = y * jax.nn.sigmoid(jnp.dot(y.astype(BF16), wglu_ref[...], preferred_element_type=F32) + bglu_ref[...])
    sn = _rms_norm_rows(z, gs_ref[...])
    mix = (jnp.dot(a.astype(BF16), wo_ref[0:aw, :], preferred_element_type=F32)
           + jnp.dot(sn.astype(BF16), wo_ref[aw:, :], preferred_element_type=F32))
    gate = mod_ref[0, 5:6, :]
    r = ALPHA * x_ref[...] + (1.0 + gate) * mix
    o_ref[...] = _layer_norm_rows(r, lng_ref[...], lnb_ref[...])


def _mix(attn2d, y_rows, x2d, mod, w_glu, b_glu, g_attn, g_ssm, w_out, ln_g, ln_b, *, seq, tm=512):
    n, d = x2d.shape
    aw = attn2d.shape[1]
    sw = y_rows.shape[1] // SUBLANES
    tiles_per_batch = seq // tm
    const = lambda m: (0, 0)
    return pl.pallas_call(
        _mix_kernel,
        grid=(n // tm,),
        in_specs=[pl.BlockSpec((tm, aw), lambda m: (m, 0)),
                  pl.BlockSpec((tm, sw), lambda m: (m % tiles_per_batch, m // tiles_per_batch)),
                  pl.BlockSpec((tm, d), lambda m: (m, 0)),
                  pl.BlockSpec((1, N_MOD, d), lambda m: (m // tiles_per_batch, 0, 0)),
                  pl.BlockSpec((sw, sw), const),
                  pl.BlockSpec((1, sw), const),
                  pl.BlockSpec((1, aw), const),
                  pl.BlockSpec((1, sw), const),
                  pl.BlockSpec((aw + sw, d), const),
                  pl.BlockSpec((1, d), const),
                  pl.BlockSpec((1, d), const)],
        out_specs=pl.BlockSpec((tm, d), lambda m: (m, 0)),
        out_shape=jax.ShapeDtypeStruct((n, d), F32),
        compiler_params=_params(("parallel",)),
        name="mix",
    )(attn2d, y_rows, x2d, mod, w_glu, b_glu.reshape(1, sw), g_attn.reshape(1, aw), g_ssm.reshape(1, sw),
      w_out, ln_g.reshape(1, d), ln_b.reshape(1, d))


def kernel(x, c, w_ada, b_ada, ffn1_w_gate, ffn1_w_up, ffn1_w_down, ln1_g, ln1_b, w_in, attn_norm_g, ssm_lambda_re, ssm_lambda_im, ssm_log_dt, ssm_b_re, ssm_b_im, ssm_c_re, ssm_c_im, ssm_d, ssm_w_glu, ssm_b_glu, ssm_norm_g, w_out, ln2_g, ln2_b, ffn2_w_gate, ffn2_w_up, ffn2_w_down, ln3_g, ln3_b):
    b, s, d = x.shape
    assert w_ada.shape[0] == DEPTH
    slopes = 2.0 ** (-(8.0 / N_HEADS) * jnp.arange(1, N_HEADS + 1, dtype=F32))
    c_pad = jnp.pad(c, ((0, SUBLANES - b), (0, 0)))
    x2d = x.reshape(b * s, d)
    for l in range(DEPTH):
        mod = _adaln(c_pad, w_ada[l], b_ada[l])[:b].reshape(b, N_MOD, d)

        x2d = _ffn(x2d, mod, ffn1_w_gate[l].astype(BF16), ffn1_w_up[l].astype(BF16), ffn1_w_down[l].astype(BF16),
                   ln1_g[l], ln1_b[l], mod_base=0, seq=s)

        q, k, v, u = _inproj(x2d, mod, w_in[l].astype(BF16), seq=s)
        aw = q.shape[1]
        attn = _attention(q.reshape(b, s, aw), k.reshape(b, s, aw), v.reshape(b, s, aw), slopes)

        bb, cc, a_re, a_im = _ssm_tables(ssm_lambda_re[l], ssm_lambda_im[l], ssm_log_dt[l],
                                         ssm_b_re[l], ssm_b_im[l], ssm_c_re[l], ssm_c_im[l])
        sw = u.shape[1] // SUBLANES
        y = _ssm(u.reshape(s, SUBLANES, sw), bb, cc, a_re, a_im, ssm_d[l])

        x2d = _mix(attn.reshape(b * s, aw), y.reshape(s, SUBLANES * sw), x2d, mod, ssm_w_glu[l].astype(BF16), ssm_b_glu[l],
                   attn_norm_g[l], ssm_norm_g[l], w_out[l].astype(BF16), ln2_g[l], ln2_b[l], seq=s)

        x2d = _ffn(x2d, mod, ffn2_w_gate[l].astype(BF16), ffn2_w_up[l].astype(BF16), ffn2_w_down[l].astype(BF16),
                   ln3_g[l], ln3_b[l], mod_base=6, seq=s)
    return x2d.reshape(b, s, d)
```

```python
import functools
import math

import jax
import jax.numpy as jnp
from jax import lax
from jax.experimental import pallas as pl
from jax.experimental.pallas import tpu as pltpu

F32 = jnp.float32
BF16 = jnp.bfloat16

N_HEADS = 8
HEAD_DIM = 128
MOBA_BLOCK = 256
MOBA_TOPK = 3
SSM_GROUP = 16
SSM_STATE = 64
N_MOD = 9
LN_EPS = 1e-5
NEG_INF = -1e30
LOG2E = math.log2(math.e)
DEPTH = 1
ALPHA = (2.0 * DEPTH) ** 0.25
Q_PRESCALE = HEAD_DIM ** -0.5 * LOG2E

LANES = 128
SUBLANES = 8
VMEM_LIMIT_BYTES = 56 * 1024 * 1024

SLAB = LANES
HALF_STATES = 256


def _params(sem):
    return pltpu.CompilerParams(dimension_semantics=sem, vmem_limit_bytes=VMEM_LIMIT_BYTES)


def _layer_norm_rows(r, g, b):
    mu = jnp.mean(r, axis=-1, keepdims=True)
    d = r - mu
    var = jnp.mean(d * d, axis=-1, keepdims=True)
    return d * lax.rsqrt(var + LN_EPS) * g + b


def _rms_norm_rows(y, g):
    return y * lax.rsqrt(jnp.mean(y * y, axis=-1, keepdims=True) + LN_EPS) * g


def _adaln_kernel(c_ref, w_ref, b_ref, o_ref):
    c = c_ref[...]
    c_act = (c * jax.nn.sigmoid(c)).astype(BF16)
    o_ref[...] = jnp.dot(c_act, w_ref[...].astype(BF16), preferred_element_type=F32) + b_ref[...]


def _adaln(c_pad, w_ada, b_ada, tn=1024):
    rows, d = c_pad.shape
    n = w_ada.shape[1]
    return pl.pallas_call(
        _adaln_kernel,
        grid=(n // tn,),
        in_specs=[pl.BlockSpec((rows, d), lambda j: (0, 0)),
                  pl.BlockSpec((d, tn), lambda j: (0, j)),
                  pl.BlockSpec((1, tn), lambda j: (0, j))],
        out_specs=pl.BlockSpec((rows, tn), lambda j: (0, j)),
        out_shape=jax.ShapeDtypeStruct((rows, n), F32),
        compiler_params=_params(("arbitrary",)),
        name="adaln",
    )(c_pad, w_ada, b_ada.reshape(1, n))


def _ffn_kernel(x_ref, mod_ref, wg_ref, wu_ref, wd_ref, lng_ref, lnb_ref, o_ref, u_scr, acc_scr, *, mod_base):
    f = pl.program_id(1)

    @pl.when(f == 0)
    def _():
        shift = mod_ref[0, mod_base:mod_base + 1, :]
        scale = mod_ref[0, mod_base + 1:mod_base + 2, :]
        u_scr[...] = (x_ref[...] * (1.0 + scale) + shift).astype(BF16)
        acc_scr[...] = jnp.zeros_like(acc_scr)

    u = u_scr[...]
    g = jnp.dot(u, wg_ref[...], preferred_element_type=F32)
    up = jnp.dot(u, wu_ref[...], preferred_element_type=F32)
    h = (g * jax.nn.sigmoid(g)) * up
    acc_scr[...] += jnp.dot(h.astype(BF16), wd_ref[...], preferred_element_type=F32)

    @pl.when(f == pl.num_programs(1) - 1)
    def _():
        gate = mod_ref[0, mod_base + 2:mod_base + 3, :]
        r = ALPHA * x_ref[...] + (0.5 * (1.0 + gate)) * acc_scr[...]
        o_ref[...] = _layer_norm_rows(r, lng_ref[...], lnb_ref[...])


def _ffn(x2d, mod, wg, wu, wd, ln_g, ln_b, *, mod_base, seq, tm=512, tf=512):
    n, d = x2d.shape
    dff = wg.shape[1]
    tiles_per_batch = seq // tm
    return pl.pallas_call(
        functools.partial(_ffn_kernel, mod_base=mod_base),
        grid=(n // tm, dff // tf),
        in_specs=[pl.BlockSpec((tm, d), lambda m, f: (m, 0)),
                  pl.BlockSpec((1, N_MOD, d), lambda m, f: (m // tiles_per_batch, 0, 0)),
                  pl.BlockSpec((d, tf), lambda m, f: (0, f)),
                  pl.BlockSpec((d, tf), lambda m, f: (0, f)),
                  pl.BlockSpec((tf, d), lambda m, f: (f, 0)),
                  pl.BlockSpec((1, d), lambda m, f: (0, 0)),
                  pl.BlockSpec((1, d), lambda m, f: (0, 0))],
        out_specs=pl.BlockSpec((tm, d), lambda m, f: (m, 0)),
        out_shape=jax.ShapeDtypeStruct((n, d), F32),
        scratch_shapes=[pltpu.VMEM((tm, d), BF16), pltpu.VMEM((tm, d), F32)],
        compiler_params=_params(("parallel", "arbitrary")),
        name="ffn",
    )(x2d, mod, wg, wu, wd, ln_g.reshape(1, d), ln_b.reshape(1, d))


def _inproj_kernel(x_ref, mod_ref, w_ref, u_init_ref, q_ref, k_ref, v_ref, u_ref, xm_scr):
    del u_init_ref
    j = pl.program_id(1)

    @pl.when(j == 0)
    def _():
        shift = mod_ref[0, 3:4, :]
        scale = mod_ref[0, 4:5, :]
        xm_scr[...] = (x_ref[...] * (1.0 + scale) + shift).astype(BF16)

    h = jnp.dot(xm_scr[...], w_ref[...], preferred_element_type=F32)
    width = q_ref.shape[1]

    @pl.when(j == 0)
    def _():
        q_ref[...] = (h[:, :width] * Q_PRESCALE).astype(BF16)
        k_ref[...] = h[:, width:].astype(BF16)

    @pl.when(j == 1)
    def _():
        v_ref[...] = h[:, :width].astype(BF16)
        u_ref[...] = h[:, width:]


def _inproj(x2d, mod, w_in, *, seq, tm=512):
    n, d = x2d.shape
    width = w_in.shape[1] // 4
    tiles_per_batch = seq // tm
    assert n // seq <= SUBLANES // 2
    out_spec = pl.BlockSpec((tm, width), lambda m, j: (m, 0))
    u_spec = pl.BlockSpec((tm, width), lambda m, j: (m % tiles_per_batch, m // tiles_per_batch))
    u_zeros = jnp.zeros((seq, SUBLANES * width), F32)
    return pl.pallas_call(
        _inproj_kernel,
        grid=(n // tm, 2),
        in_specs=[pl.BlockSpec((tm, d), lambda m, j: (m, 0)),
                  pl.BlockSpec((1, N_MOD, d), lambda m, j: (m // tiles_per_batch, 0, 0)),
                  pl.BlockSpec((d, 2 * width), lambda m, j: (0, j)),
                  pl.BlockSpec(memory_space=pl.ANY)],
        out_specs=[out_spec, out_spec, out_spec, u_spec],
        out_shape=[jax.ShapeDtypeStruct((n, width), BF16)] * 3 + [jax.ShapeDtypeStruct((seq, SUBLANES * width), F32)],
        input_output_aliases={3: 3},
        scratch_shapes=[pltpu.VMEM((tm, d), BF16)],
        compiler_params=_params(("parallel", "arbitrary")),
        name="inproj",
    )(x2d, mod, w_in, u_zeros)


def _dot_nt(a, b):
    return lax.dot_general(a, b, (((1,), (1,)), ((), ())), preferred_element_type=F32)


def _attn_kernel(slopes_ref, qa_ref, qb_ref, k_ref, v_ref, o_ref, kmean_scr, vt_scr, base_scr, q_scr, pk_scr, m_scr,
                 acc_scr, sa_scr, sb_scr, pa_scr, pb_scr, *, n_blocks, hg):
    g = pl.program_id(1)
    p = pl.program_id(2)
    blk = MOBA_BLOCK
    dh = HEAD_DIM
    slope = [slopes_ref[g * hg + h] for h in range(hg)]
    hcols = [slice(h * dh, (h + 1) * dh) for h in range(hg)]
    q_refs = (qa_ref, qb_ref)
    tile_blk = (p, n_blocks - 1 - p)

    key_id = lax.broadcasted_iota(jnp.int32, (blk, blk), 0)
    qry_id = lax.broadcasted_iota(jnp.int32, (blk, blk), 1)
    rel = (qry_id - key_id).astype(F32)

    @pl.when(p == 0)
    def _():
        row_id = lax.broadcasted_iota(jnp.int32, (vt_scr.shape[1] - dh, vt_scr.shape[2]), 0)
        ones_row = jnp.where(row_id == 0, 1.0, 0.0).astype(BF16)
        for h in range(hg):
            base_scr[h] = (-LOG2E * slope[h]) * rel
            vt_scr[h, dh:, :] = ones_row
            for j in range(n_blocks):
                rows = slice(j * blk, (j + 1) * blk)
                kmean_scr[h, j:j + 1, :] = jnp.mean(k_ref[0, rows, hcols[h]].astype(F32), axis=0, keepdims=True)
                vt_scr[h, 0:dh, rows] = v_ref[0, rows, hcols[h]].astype(F32).T.astype(BF16)

    def item(n):
        tile = (n >= p).astype(jnp.int32)
        return tile, n - p * tile, jnp.where(tile == 0, tile_blk[0], tile_blk[1])

    def scores_into(n, s_scr):
        tile, j, _ = item(n)
        keys = pl.ds(pl.multiple_of(j * blk, blk), blk)
        for h in range(hg):
            s_scr[h] = _dot_nt(k_ref[0, keys, hcols[h]], q_scr[tile, h])

    def softmax_stage(n, s_scr, p_scr):
        tile, j, i_blk = item(n)
        j_f = j.astype(F32)
        i_f = i_blk.astype(F32)
        alphas = []
        for h in range(hg):
            selected = ((pk_scr[tile, h, 0:1, :] == j_f) | (pk_scr[tile, h, 1:2, :] == j_f)
                        | (pk_scr[tile, h, 2:3, :] == j_f))
            row_bias = jnp.where(selected, (j_f - i_f) * (float(blk) * LOG2E * slope[h]), NEG_INF)
            x = s_scr[h] + base_scr[h]
            m_old = m_scr[tile, h]
            m_new = jnp.maximum(m_old, jnp.max(x, axis=0, keepdims=True) + row_bias)
            p_scr[h] = jnp.exp2(x - (m_new - row_bias)).astype(BF16)
            m_scr[tile, h] = m_new
            alphas.append(jnp.exp2(m_old - m_new))
        return tuple(alphas)

    def accumulate(n, p_scr, alphas):
        tile, j, _ = item(n)
        keys = pl.ds(pl.multiple_of(j * blk, blk), blk)
        pv = [jnp.dot(vt_scr[h, :, keys], p_scr[h], preferred_element_type=F32) for h in range(hg)]
        for h in range(hg):
            acc_scr[tile, h] = alphas[h] * acc_scr[tile, h] + pv[h]

    for t in range(2):
        for h in range(hg):
            q_scr[t, h] = q_refs[t][0, 0, :, hcols[h]]
    scores_into(0, sa_scr)
    own = [pl.ds(pl.multiple_of(tile_blk[t] * blk, blk), blk) for t in range(2)]
    gates = [[_dot_nt(kmean_scr[h].astype(BF16), q_scr[t, h]) for h in range(hg)] for t in range(2)]
    s_own = [[_dot_nt(k_ref[0, own[t], hcols[h]], q_scr[t, h]) for h in range(hg)] for t in range(2)]
    p_own = [[None] * hg for _ in range(2)]
    for t in range(2):
        i_f = tile_blk[t].astype(F32)
        for h in range(hg):
            blk_id = lax.broadcasted_iota(jnp.int32, gates[t][h].shape, 0).astype(F32)
            work = jnp.where(blk_id < i_f, gates[t][h], NEG_INF)
            for r in range(MOBA_TOPK):
                mx = jnp.max(work, axis=0, keepdims=True)
                idx = jnp.min(jnp.where(work == mx, blk_id, float(n_blocks)), axis=0, keepdims=True)
                work = jnp.where(blk_id == idx, -jnp.inf, work)
                pk_scr[t, h, r:r + 1, :] = jnp.where(idx < i_f, idx, -1.0)
            s = jnp.where(rel >= 0.0, s_own[t][h] + base_scr[h], NEG_INF)
            m = jnp.max(s, axis=0, keepdims=True)
            m_scr[t, h] = m
            p_own[t][h] = jnp.exp2(s - m).astype(BF16)
    for t in range(2):
        pv_own = [jnp.dot(vt_scr[h, :, own[t]], p_own[t][h], preferred_element_type=F32) for h in range(hg)]
        for h in range(hg):
            acc_scr[t, h] = pv_own[h]

    s_slots = (sa_scr, sb_scr)
    p_slots = (pa_scr, pb_scr)
    a_prev = None
    for n in range(n_blocks - 1):
        if n + 1 < n_blocks - 1:
            scores_into(n + 1, s_slots[(n + 1) % 2])
        a_cur = softmax_stage(n, s_slots[n % 2], p_slots[n % 2])
        if n > 0:
            accumulate(n - 1, p_slots[(n - 1) % 2], a_prev)
        a_prev = a_cur
    accumulate(n_blocks - 2, p_slots[(n_blocks - 2) % 2], a_prev)
    for t in range(2):
        for h in range(hg):
            acc = acc_scr[t, h]
            o_ref[0, t, 0, :, hcols[h]] = (acc[0:dh] / acc[dh:dh + 1]).T


def _attention(q, k, v, slopes, hg=4):
    b, s, width = q.shape
    n_heads = width // HEAD_DIM
    n_blocks = s // MOBA_BLOCK
    assert n_blocks % 2 == 0 and n_heads % hg == 0
    n_steps = n_blocks // 2
    gw = hg * HEAD_DIM
    ones_rows = 16
    kv_spec = pl.BlockSpec((1, s, gw), lambda bi, g, p: (bi, 0, g))
    slot = lambda dims, dt: pltpu.VMEM(dims, dt)
    return pl.pallas_call(
        functools.partial(_attn_kernel, n_blocks=n_blocks, hg=hg),
        grid=(b, n_heads // hg, n_steps),
        in_specs=[pl.BlockSpec(memory_space=pltpu.SMEM),
                  pl.BlockSpec((1, 1, MOBA_BLOCK, gw), lambda bi, g, p: (bi, p, 0, g)),
                  pl.BlockSpec((1, 1, MOBA_BLOCK, gw), lambda bi, g, p: (bi, n_blocks - 1 - p, 0, g)),
                  kv_spec, kv_spec],
        out_specs=pl.BlockSpec((1, 2, 1, MOBA_BLOCK, gw), lambda bi, g, p: (bi, 0, p, 0, g)),
        out_shape=jax.ShapeDtypeStruct((b, 2, n_steps, MOBA_BLOCK, width), F32),
        scratch_shapes=[slot((hg, n_blocks, HEAD_DIM), F32),
                        slot((hg, HEAD_DIM + ones_rows, s), BF16),
                        slot((hg, MOBA_BLOCK, MOBA_BLOCK), F32),
                        slot((2, hg, MOBA_BLOCK, HEAD_DIM), BF16),
                        slot((2, hg, SUBLANES, MOBA_BLOCK), F32),
                        slot((2, hg, 1, MOBA_BLOCK), F32),
                        slot((2, hg, HEAD_DIM + ones_rows, MOBA_BLOCK), F32),
                        slot((hg, MOBA_BLOCK, MOBA_BLOCK), F32), slot((hg, MOBA_BLOCK, MOBA_BLOCK), F32),
                        slot((hg, MOBA_BLOCK, MOBA_BLOCK), BF16), slot((hg, MOBA_BLOCK, MOBA_BLOCK), BF16)],
        compiler_params=_params(("parallel", "parallel", "arbitrary")),
        name="moba_attn",
    )(slopes, q.reshape(b, n_blocks, MOBA_BLOCK, width), q.reshape(b, n_blocks, MOBA_BLOCK, width), k, v)


def _ssm_tables(lam_re, lam_im, log_dt, b_re, b_im, c_re, c_im):
    g_total, p = lam_re.shape
    n_slab = g_total * SSM_GROUP // SLAB
    gps = SLAB // SSM_GROUP
    gph = gps // 2
    dt = jnp.exp(log_dt)[:, None]
    decay = jnp.exp(lam_re * dt)
    ab_re = decay * jnp.cos(lam_im * dt)
    ab_im = decay * jnp.sin(lam_im * dt)
    den = lam_re * lam_re + lam_im * lam_im
    coef_re = ((ab_re - 1.0) * lam_re + ab_im * lam_im) / den
    coef_im = (ab_im * lam_re - (ab_re - 1.0) * lam_im) / den
    bb_re = coef_re[..., None] * b_re - coef_im[..., None] * b_im
    bb_im = coef_re[..., None] * b_im + coef_im[..., None] * b_re
    eye = jnp.eye(gps, dtype=F32)

    bbri = jnp.stack([bb_re, bb_im], axis=0).reshape(2, n_slab, gps, p, SSM_GROUP)
    bb = jnp.einsum('rsgph,gk->sghkrp', bbri, eye)
    bb = bb.reshape(n_slab, gps, SSM_GROUP, 2, gph, 2, p).transpose(0, 1, 2, 3, 5, 4, 6)
    bb = bb.reshape(n_slab, SLAB, 2, 2 * HALF_STATES).transpose(0, 2, 1, 3).reshape(n_slab, 2 * SLAB, 2 * HALF_STATES)

    ccri = jnp.stack([c_re, -c_im], axis=0).reshape(2, n_slab, gps, SSM_GROUP, p)
    cc = jnp.einsum('rsghp,gk->skrpgh', ccri, eye)
    cc = cc.reshape(n_slab, 2, gph, 2, p, gps, SSM_GROUP).transpose(0, 1, 3, 2, 4, 5, 6)
    cc = cc.reshape(n_slab, 2, 2 * HALF_STATES, SLAB).transpose(0, 2, 1, 3).reshape(n_slab, 2 * HALF_STATES, 2 * SLAB)

    def rows(a):
        a = a.reshape(n_slab, 2, 1, HALF_STATES)
        return jnp.broadcast_to(a, (n_slab, 2, 4, HALF_STATES)).reshape(n_slab, SUBLANES, HALF_STATES)

    return bb.astype(BF16), cc.astype(BF16), rows(ab_re), rows(ab_im)


def _ssm_kernel(u_ref, bb_ref, cc_ref, are_ref, aim_ref, d_ref, y_ref, x_scr, carry_scr, *, t_chunk):
    c = pl.program_id(0)
    s = pl.program_id(1)
    rows = SUBLANES * t_chunk
    hs = HALF_STATES
    n_cg = 2 * hs // LANES
    n_re = hs // LANES
    half_rows = SUBLANES // 2

    u3 = u_ref[...]
    u_lo = u3.reshape(rows, SLAB)
    u_hi = pltpu.roll(u3, half_rows, 1).reshape(rows, SLAB)
    lhs = jnp.concatenate([u_lo, u_hi], axis=1).astype(BF16)
    row_parts = [slice(k * rows // 2, (k + 1) * rows // 2) for k in range(2)]
    for part in row_parts:
        bu = jnp.dot(lhs[part], bb_ref[0], preferred_element_type=F32)
        for g in range(n_cg):
            x_scr[g, part, :] = bu[:, g * LANES:(g + 1) * LANES]

    @pl.when(c == 0)
    def _():
        carry_scr[s] = jnp.zeros((n_cg, SUBLANES, LANES), F32)

    a_re = [are_ref[0, :, g * LANES:(g + 1) * LANES] for g in range(n_re)]
    a_im = [aim_ref[0, :, g * LANES:(g + 1) * LANES] for g in range(n_re)]
    x0 = tuple(carry_scr[s, g] for g in range(n_cg))

    def step(t, carry):
        idx = pl.ds(pl.multiple_of(t * SUBLANES, SUBLANES), SUBLANES)
        new = [None] * n_cg
        for g in range(n_re):
            x_re, x_im = carry[g], carry[n_re + g]
            new[g] = a_re[g] * x_re - a_im[g] * x_im + x_scr[g, idx, :]
            new[n_re + g] = a_re[g] * x_im + a_im[g] * x_re + x_scr[n_re + g, idx, :]
        for g in range(n_cg):
            x_scr[g, idx, :] = new[g]
        return tuple(new)

    x_fin = lax.fori_loop(0, t_chunk, step, x0, unroll=8)
    for g in range(n_cg):
        carry_scr[s, g] = x_fin[g]

    yy = jnp.concatenate(
        [jnp.dot(jnp.concatenate([x_scr[g, part, :] for g in range(n_cg)], axis=1).astype(BF16), cc_ref[0],
                 preferred_element_type=F32) for part in row_parts], axis=0)
    y_half1 = pltpu.roll(yy[:, SLAB:].reshape(t_chunk, SUBLANES, SLAB), half_rows, 1)
    y3 = yy[:, :SLAB].reshape(t_chunk, SUBLANES, SLAB) + y_half1
    batch_rows = lax.broadcasted_iota(jnp.int32, y3.shape, 1) < half_rows
    y_ref[...] = jnp.where(batch_rows, y3, 0.0) + d_ref[...] * u3


def _ssm(u, bb, cc, a_re, a_im, d_skip, t_chunk=256):
    s, r, width = u.shape
    assert r == SUBLANES
    n_slab = width // SLAB
    hs = HALF_STATES
    return pl.pallas_call(
        functools.partial(_ssm_kernel, t_chunk=t_chunk),
        grid=(s // t_chunk, n_slab),
        in_specs=[pl.BlockSpec((t_chunk, SUBLANES, SLAB), lambda c, sl: (c, 0, sl)),
                  pl.BlockSpec((1, 2 * SLAB, 2 * hs), lambda c, sl: (sl, 0, 0)),
                  pl.BlockSpec((1, 2 * hs, 2 * SLAB), lambda c, sl: (sl, 0, 0)),
                  pl.BlockSpec((1, SUBLANES, hs), lambda c, sl: (sl, 0, 0)),
                  pl.BlockSpec((1, SUBLANES, hs), lambda c, sl: (sl, 0, 0)),
                  pl.BlockSpec((1, SLAB), lambda c, sl: (0, sl))],
        out_specs=pl.BlockSpec((t_chunk, SUBLANES, SLAB), lambda c, sl: (c, 0, sl)),
        out_shape=jax.ShapeDtypeStruct((s, SUBLANES, width), F32),
        scratch_shapes=[pltpu.VMEM((2 * hs // LANES, SUBLANES * t_chunk, LANES), F32),
                        pltpu.VMEM((n_slab, 2 * hs // LANES, SUBLANES, LANES), F32)],
        compiler_params=_params(("arbitrary", "arbitrary")),
        name="s5_scan",
    )(u, bb, cc, a_re, a_im, d_skip.reshape(1, width))


def _mix_kernel(a0_ref, a1_ref, y_ref, x_ref, mod_ref, wglu_ref, bglu_ref, ga_ref, gs_ref, wo_ref, lng_ref, lnb_ref,
                o_ref):
    aw = a0_ref.shape[1]
    a = _rms_norm_rows(jnp.concatenate([a0_ref[...], a1_ref[...]], axis=0), ga_ref[...])
    y = jax.nn.gelu(y_ref[...])
    z = y * jax.nn.sigmoid(jnp.dot(y.astype(BF16), wglu_ref[...], preferred_element_type=F32) + bglu_ref[...])
    sn = _rms_norm_rows(z, gs_ref[...])
    mix = (jnp.dot(a.astype(BF16), wo_ref[0:aw, :], preferred_element_type=F32)
           + jnp.dot(sn.astype(BF16), wo_ref[aw:, :], preferred_element_type=F32))
    gate = mod_ref[0, 5:6, :]
    r = ALPHA * x_ref[...] + (1.0 + gate) * mix
    o_ref[...] = _layer_norm_rows(r, lng_ref[...], lnb_ref[...])


def _mix(attn, y_rows, x2d, mod, w_glu, b_glu, g_attn, g_ssm, w_out, ln_g, ln_b, *, seq):
    n, d = x2d.shape
    _, _, n_steps, blk, aw = attn.shape
    tm = 2 * blk
    sw = y_rows.shape[1] // SUBLANES
    tiles_per_batch = seq // tm
    const = lambda m: (0, 0)

    def attn_spec(e):
        def index(m):
            i = 2 * (m % tiles_per_batch) + e
            folded = i >= n_steps
            return (m // tiles_per_batch, folded.astype(jnp.int32), jnp.where(folded, 2 * n_steps - 1 - i, i), 0, 0)
        return pl.BlockSpec((None, None, None, blk, aw), index)

    return pl.pallas_call(
        _mix_kernel,
        grid=(n // tm,),
        in_specs=[attn_spec(0), attn_spec(1),
                  pl.BlockSpec((tm, sw), lambda m: (m % tiles_per_batch, m // tiles_per_batch)),
                  pl.BlockSpec((tm, d), lambda m: (m, 0)),
                  pl.BlockSpec((1, N_MOD, d), lambda m: (m // tiles_per_batch, 0, 0)),
                  pl.BlockSpec((sw, sw), const),
                  pl.BlockSpec((1, sw), const),
                  pl.BlockSpec((1, aw), const),
                  pl.BlockSpec((1, sw), const),
                  pl.BlockSpec((aw + sw, d), const),
                  pl.BlockSpec((1, d), const),
                  pl.BlockSpec((1, d), const)],
        out_specs=pl.BlockSpec((tm, d), lambda m: (m, 0)),
        out_shape=jax.ShapeDtypeStruct((n, d), F32),
        compiler_params=_params(("parallel",)),
        name="mix",
    )(attn, attn, y_rows, x2d, mod, w_glu, b_glu.reshape(1, sw), g_attn.reshape(1, aw), g_ssm.reshape(1, sw),
      w_out, ln_g.reshape(1, d), ln_b.reshape(1, d))


def kernel(x, c, w_ada, b_ada, ffn1_w_gate, ffn1_w_up, ffn1_w_down, ln1_g, ln1_b, w_in, attn_norm_g, ssm_lambda_re, ssm_lambda_im, ssm_log_dt, ssm_b_re, ssm_b_im, ssm_c_re, ssm_c_im, ssm_d, ssm_w_glu, ssm_b_glu, ssm_norm_g, w_out, ln2_g, ln2_b, ffn2_w_gate, ffn2_w_up, ffn2_w_down, ln3_g, ln3_b):
    b, s, d = x.shape
    assert w_ada.shape[0] == DEPTH
    slopes = 2.0 ** (-(8.0 / N_HEADS) * jnp.arange(1, N_HEADS + 1, dtype=F32))
    c_pad = jnp.pad(c, ((0, SUBLANES - b), (0, 0)))
    x2d = x.reshape(b * s, d)
    for l in range(DEPTH):
        mod = _adaln(c_pad, w_ada[l], b_ada[l])[:b].reshape(b, N_MOD, d)

        x2d = _ffn(x2d, mod, ffn1_w_gate[l].astype(BF16), ffn1_w_up[l].astype(BF16), ffn1_w_down[l].astype(BF16),
                   ln1_g[l], ln1_b[l], mod_base=0, seq=s)

        q, k, v, u = _inproj(x2d, mod, w_in[l].astype(BF16), seq=s)
        aw = q.shape[1]
        attn = _attention(q.reshape(b, s, aw), k.reshape(b, s, aw), v.reshape(b, s, aw), slopes)

        bb, cc, a_re, a_im = _ssm_tables(ssm_lambda_re[l], ssm_lambda_im[l], ssm_log_dt[l],
                                         ssm_b_re[l], ssm_b_im[l], ssm_c_re[l], ssm_c_im[l])
        sw = u.shape[1] // SUBLANES
        y = _ssm(u.reshape(s, SUBLANES, sw), bb, cc, a_re, a_im, ssm_d[l])

        x2d = _mix(attn, y.reshape(s, SUBLANES * sw), x2d, mod, ssm_w_glu[l].astype(BF16), ssm_b_glu[l],
                   attn_norm_g[l], ssm_norm_g[l], w_out[l].astype(BF16), ln2_g[l], ln2_b[l], seq=s)

        x2d = _ffn(x2d, mod, ffn2_w_gate[l].astype(BF16), ffn2_w_up[l].astype(BF16), ffn2_w_down[l].astype(BF16),
                   ln3_g[l], ln3_b[l], mod_base=6, seq=s)
    return x2d.reshape(b, s, d)
```

```python
import functools
import math

import jax
import jax.numpy as jnp
from jax import lax
from jax.experimental import pallas as pl
from jax.experimental.pallas import tpu as pltpu

F32 = jnp.float32
BF16 = jnp.bfloat16

N_HEADS = 8
HEAD_DIM = 128
MOBA_BLOCK = 256
MOBA_TOPK = 3
SSM_GROUP = 16
SSM_STATE = 64
N_MOD = 9
LN_EPS = 1e-5
NEG_INF = -1e30
LOG2E = math.log2(math.e)
DEPTH = 1
ALPHA = (2.0 * DEPTH) ** 0.25
Q_PRESCALE = HEAD_DIM ** -0.5 * LOG2E

LANES = 128
SUBLANES = 8
VMEM_LIMIT_BYTES = 56 * 1024 * 1024

SLAB = LANES
HALF_STATES = 256
SCAN_SUBCHUNKS = 4


def _params(sem):
    return pltpu.CompilerParams(dimension_semantics=sem, vmem_limit_bytes=VMEM_LIMIT_BYTES)


def _residual_layer_norm(x, y, coef, g, b):
    r = x + (coef * (1.0 / ALPHA)) * y
    mu = jnp.mean(r, axis=-1, keepdims=True)
    d = r - mu
    var = jnp.mean(d * d, axis=-1, keepdims=True)
    return d * lax.rsqrt(var + LN_EPS / (ALPHA * ALPHA)) * g + b


def _rms_norm_rows(y, g):
    return y * lax.rsqrt(jnp.mean(y * y, axis=-1, keepdims=True) + LN_EPS) * g


def _adaln_kernel(c_ref, w_ref, b_ref, o_ref):
    c = c_ref[...]
    c_act = (c * jax.nn.sigmoid(c)).astype(BF16)
    o_ref[...] = jnp.dot(c_act, w_ref[...].astype(BF16), preferred_element_type=F32) + b_ref[...]


def _adaln(c_pad, w_ada, b_ada, tn=1024):
    rows, d = c_pad.shape
    n = w_ada.shape[1]
    return pl.pallas_call(
        _adaln_kernel,
        grid=(n // tn,),
        in_specs=[pl.BlockSpec((rows, d), lambda j: (0, 0)),
                  pl.BlockSpec((d, tn), lambda j: (0, j)),
                  pl.BlockSpec((1, tn), lambda j: (0, j))],
        out_specs=pl.BlockSpec((rows, tn), lambda j: (0, j)),
        out_shape=jax.ShapeDtypeStruct((rows, n), F32),
        compiler_params=_params(("arbitrary",)),
        name="adaln",
    )(c_pad, w_ada, b_ada.reshape(1, n))


def _ffn_kernel(*refs, mod_base, n_cast):
    x_ref, mod_ref, wg_ref, wu_ref, wd_ref, lng_ref, lnb_ref = refs[:7]
    cast_in = refs[7:7 + n_cast]
    o_ref = refs[7 + n_cast]
    cast_out = refs[8 + n_cast:8 + 2 * n_cast]
    u_scr, acc_scr = refs[8 + 2 * n_cast:]
    f = pl.program_id(1)

    @pl.when(f == 0)
    def _():
        shift = mod_ref[0, mod_base:mod_base + 1, :]
        scale = mod_ref[0, mod_base + 1:mod_base + 2, :]
        u_scr[...] = (x_ref[...] * (1.0 + scale) + shift).astype(BF16)
        acc_scr[...] = jnp.zeros_like(acc_scr)

    u = u_scr[...]
    g = jnp.dot(u, wg_ref[...], preferred_element_type=F32)
    up = jnp.dot(u, wu_ref[...], preferred_element_type=F32)
    for src, dst in zip(cast_in, cast_out):
        dst[...] = src[...].astype(BF16)
    h = (g * jax.nn.sigmoid(g)) * up
    acc_scr[...] += jnp.dot(h.astype(BF16), wd_ref[...], preferred_element_type=F32)

    @pl.when(f == pl.num_programs(1) - 1)
    def _():
        gate = mod_ref[0, mod_base + 2:mod_base + 3, :]
        o_ref[...] = _residual_layer_norm(x_ref[...], acc_scr[...], 0.5 * (1.0 + gate), lng_ref[...], lnb_ref[...])


def _cast_job_spec(w, n_m, n_f):
    rows, cols = w.shape
    bf16_rows = 2 * SUBLANES
    if rows % n_m == 0 and cols % n_f == 0 and (rows // n_m) % bf16_rows == 0 and (cols // n_f) % LANES == 0:
        return pl.BlockSpec((rows // n_m, cols // n_f), lambda m, f: (m, f))
    if rows % (n_m * n_f) == 0 and (rows // (n_m * n_f)) % bf16_rows == 0:
        return pl.BlockSpec((rows // (n_m * n_f), cols), lambda m, f: (m * n_f + f, 0))
    assert rows % n_m == 0 and (rows // n_m) % bf16_rows == 0, w.shape
    return pl.BlockSpec((rows // n_m, cols), lambda m, f: (m, 0))


def _ffn(x2d, mod, wg, wu, wd, ln_g, ln_b, *, mod_base, seq, cast_jobs=(), tm=512, tf=512):
    n, d = x2d.shape
    dff = wg.shape[1]
    tiles_per_batch = seq // tm
    grid = (n // tm, dff // tf)
    cast_specs = [_cast_job_spec(w, *grid) for w in cast_jobs]
    out = pl.pallas_call(
        functools.partial(_ffn_kernel, mod_base=mod_base, n_cast=len(cast_jobs)),
        grid=grid,
        in_specs=[pl.BlockSpec((tm, d), lambda m, f: (m, 0)),
                  pl.BlockSpec((1, N_MOD, d), lambda m, f: (m // tiles_per_batch, 0, 0)),
                  pl.BlockSpec((d, tf), lambda m, f: (0, f)),
                  pl.BlockSpec((d, tf), lambda m, f: (0, f)),
                  pl.BlockSpec((tf, d), lambda m, f: (f, 0)),
                  pl.BlockSpec((1, d), lambda m, f: (0, 0)),
                  pl.BlockSpec((1, d), lambda m, f: (0, 0))] + cast_specs,
        out_specs=[pl.BlockSpec((tm, d), lambda m, f: (m, 0))] + cast_specs,
        out_shape=[jax.ShapeDtypeStruct((n, d), F32)] + [jax.ShapeDtypeStruct(w.shape, BF16) for w in cast_jobs],
        scratch_shapes=[pltpu.VMEM((tm, d), BF16), pltpu.VMEM((tm, d), F32)],
        compiler_params=_params(("arbitrary", "arbitrary")),
        name="ffn",
    )(x2d, mod, wg, wu, wd, ln_g.reshape(1, d), ln_b.reshape(1, d), *cast_jobs)
    return out[0], out[1:]


def _inproj_kernel(x_ref, mod_ref, w_ref, u_init_ref, q_ref, k_ref, v_ref, u_ref, xm_scr):
    del u_init_ref
    j = pl.program_id(1)

    @pl.when(j == 0)
    def _():
        shift = mod_ref[0, 3:4, :]
        scale = mod_ref[0, 4:5, :]
        xm_scr[...] = (x_ref[...] * (1.0 + scale) + shift).astype(BF16)

    h = jnp.dot(xm_scr[...], w_ref[...], preferred_element_type=F32)
    width = q_ref.shape[1]

    @pl.when(j == 0)
    def _():
        q_ref[...] = (h[:, :width] * Q_PRESCALE).astype(BF16)
        k_ref[...] = h[:, width:].astype(BF16)

    @pl.when(j == 1)
    def _():
        v_ref[...] = h[:, :width].astype(BF16)
        u_ref[...] = h[:, width:]


def _inproj(x2d, mod, w_in, *, seq, tm=512):
    n, d = x2d.shape
    width = w_in.shape[1] // 4
    tiles_per_batch = seq // tm
    assert n // seq <= SUBLANES // 2
    out_spec = pl.BlockSpec((tm, width), lambda m, j: (m, 0))
    u_spec = pl.BlockSpec((tm, width), lambda m, j: (m % tiles_per_batch, m // tiles_per_batch))
    u_zeros = jnp.zeros((seq, SUBLANES * width), F32)
    return pl.pallas_call(
        _inproj_kernel,
        grid=(n // tm, 2),
        in_specs=[pl.BlockSpec((tm, d), lambda m, j: (m, 0)),
                  pl.BlockSpec((1, N_MOD, d), lambda m, j: (m // tiles_per_batch, 0, 0)),
                  pl.BlockSpec((d, 2 * width), lambda m, j: (0, j)),
                  pl.BlockSpec(memory_space=pl.ANY)],
        out_specs=[out_spec, out_spec, out_spec, u_spec],
        out_shape=[jax.ShapeDtypeStruct((n, width), BF16)] * 3 + [jax.ShapeDtypeStruct((seq, SUBLANES * width), F32)],
        input_output_aliases={3: 3},
        scratch_shapes=[pltpu.VMEM((tm, d), BF16)],
        compiler_params=_params(("parallel", "arbitrary")),
        name="inproj",
    )(x2d, mod, w_in, u_zeros)


def _dot_nt(a, b):
    return lax.dot_general(a, b, (((1,), (1,)), ((), ())), preferred_element_type=F32)


def _attn_kernel(slopes_ref, qa_ref, qb_ref, k_ref, v_ref, o_ref, kmean_scr, vt_scr, base_scr, q_scr, pk_scr, m_scr,
                 acc_scr, sa_scr, sb_scr, pa_scr, pb_scr, *, n_blocks, hg):
    g = pl.program_id(1)
    p = pl.program_id(2)
    blk = MOBA_BLOCK
    dh = HEAD_DIM
    slope = [slopes_ref[g * hg + h] for h in range(hg)]
    hcols = [slice(h * dh, (h + 1) * dh) for h in range(hg)]
    q_refs = (qa_ref, qb_ref)
    tile_blk = (p, n_blocks - 1 - p)

    key_id = lax.broadcasted_iota(jnp.int32, (blk, blk), 0)
    qry_id = lax.broadcasted_iota(jnp.int32, (blk, blk), 1)
    rel = (qry_id - key_id).astype(F32)

    @pl.when(p == 0)
    def _():
        row_id = lax.broadcasted_iota(jnp.int32, (vt_scr.shape[1] - dh, vt_scr.shape[2]), 0)
        ones_row = jnp.where(row_id == 0, 1.0, 0.0).astype(BF16)
        for h in range(hg):
            base_scr[h] = (-LOG2E * slope[h]) * rel
            vt_scr[h, dh:, :] = ones_row
            for j in range(n_blocks):
                rows = slice(j * blk, (j + 1) * blk)
                kmean_scr[h, j:j + 1, :] = jnp.mean(k_ref[0, rows, hcols[h]].astype(F32), axis=0, keepdims=True)
                vt_scr[h, 0:dh, rows] = v_ref[0, rows, hcols[h]].astype(F32).T.astype(BF16)

    def item(n):
        tile = (n >= p).astype(jnp.int32)
        return tile, n - p * tile, jnp.where(tile == 0, tile_blk[0], tile_blk[1])

    def scores_into(n, s_scr):
        tile, j, _ = item(n)
        keys = pl.ds(pl.multiple_of(j * blk, blk), blk)
        for h in range(hg):
            s_scr[h] = _dot_nt(k_ref[0, keys, hcols[h]], q_scr[tile, h])

    def softmax_stage(n, s_scr, p_scr):
        tile, j, i_blk = item(n)
        j_f = j.astype(F32)
        i_f = i_blk.astype(F32)
        alphas = []
        for h in range(hg):
            selected = ((pk_scr[tile, h, 0:1, :] == j_f) | (pk_scr[tile, h, 1:2, :] == j_f)
                        | (pk_scr[tile, h, 2:3, :] == j_f))
            row_bias = jnp.where(selected, (j_f - i_f) * (float(blk) * LOG2E * slope[h]), NEG_INF)
            x = s_scr[h] + base_scr[h]
            m_old = m_scr[tile, h]
            m_new = jnp.maximum(m_old, jnp.max(x, axis=0, keepdims=True) + row_bias)
            p_scr[h] = jnp.exp2(x - (m_new - row_bias)).astype(BF16)
            m_scr[tile, h] = m_new
            alphas.append(jnp.exp2(m_old - m_new))
        return tuple(alphas)

    def accumulate(n, p_scr, alphas):
        tile, j, _ = item(n)
        keys = pl.ds(pl.multiple_of(j * blk, blk), blk)
        pv = [jnp.dot(vt_scr[h, :, keys], p_scr[h], preferred_element_type=F32) for h in range(hg)]
        for h in range(hg):
            acc_scr[tile, h] = alphas[h] * acc_scr[tile, h] + pv[h]

    for t in range(2):
        for h in range(hg):
            q_scr[t, h] = q_refs[t][0, 0, :, hcols[h]]
    scores_into(0, sa_scr)
    own = [pl.ds(pl.multiple_of(tile_blk[t] * blk, blk), blk) for t in range(2)]
    gates = [[_dot_nt(kmean_scr[h].astype(BF16), q_scr[t, h]) for h in range(hg)] for t in range(2)]
    s_own = [[_dot_nt(k_ref[0, own[t], hcols[h]], q_scr[t, h]) for h in range(hg)] for t in range(2)]
    p_own = [[None] * hg for _ in range(2)]
    for t in range(2):
        i_f = tile_blk[t].astype(F32)
        for h in range(hg):
            blk_id = lax.broadcasted_iota(jnp.int32, gates[t][h].shape, 0).astype(F32)
            work = jnp.where(blk_id < i_f, gates[t][h], NEG_INF)
            for r in range(MOBA_TOPK):
                mx = jnp.max(work, axis=0, keepdims=True)
                idx = jnp.min(jnp.where(work == mx, blk_id, float(n_blocks)), axis=0, keepdims=True)
                work = jnp.where(blk_id == idx, -jnp.inf, work)
                pk_scr[t, h, r:r + 1, :] = jnp.where(idx < i_f, idx, -1.0)
            s = jnp.where(rel >= 0.0, s_own[t][h] + base_scr[h], NEG_INF)
            m = jnp.max(s, axis=0, keepdims=True)
            m_scr[t, h] = m
            p_own[t][h] = jnp.exp2(s - m).astype(BF16)
    for t in range(2):
        pv_own = [jnp.dot(vt_scr[h, :, own[t]], p_own[t][h], preferred_element_type=F32) for h in range(hg)]
        for h in range(hg):
            acc_scr[t, h] = pv_own[h]

    s_slots = (sa_scr, sb_scr)
    p_slots = (pa_scr, pb_scr)
    a_prev = None
    for n in range(n_blocks - 1):
        if n + 1 < n_blocks - 1:
            scores_into(n + 1, s_slots[(n + 1) % 2])
        a_cur = softmax_stage(n, s_slots[n % 2], p_slots[n % 2])
        if n > 0:
            accumulate(n - 1, p_slots[(n - 1) % 2], a_prev)
        a_prev = a_cur
    accumulate(n_blocks - 2, p_slots[(n_blocks - 2) % 2], a_prev)
    for t in range(2):
        for h in range(hg):
            acc = acc_scr[t, h]
            o_ref[0, t, 0, :, hcols[h]] = (acc[0:dh] / acc[dh:dh + 1]).T


def _attention(q, k, v, slopes, hg=4):
    b, s, width = q.shape
    n_heads = width // HEAD_DIM
    n_blocks = s // MOBA_BLOCK
    assert n_blocks % 2 == 0 and n_heads % hg == 0
    n_steps = n_blocks // 2
    gw = hg * HEAD_DIM
    ones_rows = 16
    kv_spec = pl.BlockSpec((1, s, gw), lambda bi, g, p: (bi, 0, g))
    slot = lambda dims, dt: pltpu.VMEM(dims, dt)
    return pl.pallas_call(
        functools.partial(_attn_kernel, n_blocks=n_blocks, hg=hg),
        grid=(b, n_heads // hg, n_steps),
        in_specs=[pl.BlockSpec(memory_space=pltpu.SMEM),
                  pl.BlockSpec((1, 1, MOBA_BLOCK, gw), lambda bi, g, p: (bi, p, 0, g)),
                  pl.BlockSpec((1, 1, MOBA_BLOCK, gw), lambda bi, g, p: (bi, n_blocks - 1 - p, 0, g)),
                  kv_spec, kv_spec],
        out_specs=pl.BlockSpec((1, 2, 1, MOBA_BLOCK, gw), lambda bi, g, p: (bi, 0, p, 0, g)),
        out_shape=jax.ShapeDtypeStruct((b, 2, n_steps, MOBA_BLOCK, width), F32),
        scratch_shapes=[slot((hg, n_blocks, HEAD_DIM), F32),
                        slot((hg, HEAD_DIM + ones_rows, s), BF16),
                        slot((hg, MOBA_BLOCK, MOBA_BLOCK), F32),
                        slot((2, hg, MOBA_BLOCK, HEAD_DIM), BF16),
                        slot((2, hg, SUBLANES, MOBA_BLOCK), F32),
                        slot((2, hg, 1, MOBA_BLOCK), F32),
                        slot((2, hg, HEAD_DIM + ones_rows, MOBA_BLOCK), F32),
                        slot((hg, MOBA_BLOCK, MOBA_BLOCK), F32), slot((hg, MOBA_BLOCK, MOBA_BLOCK), F32),
                        slot((hg, MOBA_BLOCK, MOBA_BLOCK), BF16), slot((hg, MOBA_BLOCK, MOBA_BLOCK), BF16)],
        compiler_params=_params(("parallel", "parallel", "arbitrary")),
        name="moba_attn",
    )(slopes, q.reshape(b, n_blocks, MOBA_BLOCK, width), q.reshape(b, n_blocks, MOBA_BLOCK, width), k, v)


def _ssm_tables(lam_re, lam_im, log_dt, b_re, b_im, c_re, c_im):
    g_total, p = lam_re.shape
    n_slab = g_total * SSM_GROUP // SLAB
    gps = SLAB // SSM_GROUP
    gph = gps // 2
    dt = jnp.exp(log_dt)[:, None]
    decay = jnp.exp(lam_re * dt)
    ab_re = decay * jnp.cos(lam_im * dt)
    ab_im = decay * jnp.sin(lam_im * dt)
    den = lam_re * lam_re + lam_im * lam_im
    coef_re = ((ab_re - 1.0) * lam_re + ab_im * lam_im) / den
    coef_im = (ab_im * lam_re - (ab_re - 1.0) * lam_im) / den
    bb_re = coef_re[..., None] * b_re - coef_im[..., None] * b_im
    bb_im = coef_re[..., None] * b_im + coef_im[..., None] * b_re
    eye = jnp.eye(gps, dtype=F32)

    bbri = jnp.stack([bb_re, bb_im], axis=0).reshape(2, n_slab, gps, p, SSM_GROUP)
    bb = jnp.einsum('rsgph,gk->sghkrp', bbri, eye)
    bb = bb.reshape(n_slab, gps, SSM_GROUP, 2, gph, 2, p).transpose(0, 1, 2, 3, 5, 4, 6)
    bb = bb.reshape(n_slab, SLAB, 2, 2 * HALF_STATES).transpose(0, 2, 1, 3).reshape(n_slab, 2 * SLAB, 2 * HALF_STATES)

    ccri = jnp.stack([c_re, -c_im], axis=0).reshape(2, n_slab, gps, SSM_GROUP, p)
    cc = jnp.einsum('rsghp,gk->skrpgh', ccri, eye)
    cc = cc.reshape(n_slab, 2, gph, 2, p, gps, SSM_GROUP).transpose(0, 1, 3, 2, 4, 5, 6)
    cc = cc.reshape(n_slab, 2, 2 * HALF_STATES, SLAB).transpose(0, 2, 1, 3).reshape(n_slab, 2 * HALF_STATES, 2 * SLAB)

    def rows(a):
        a = a.reshape(n_slab, 2, 1, HALF_STATES)
        return jnp.broadcast_to(a, (n_slab, 2, 4, HALF_STATES)).reshape(n_slab, SUBLANES, HALF_STATES)

    return bb.astype(BF16), cc.astype(BF16), rows(ab_re), rows(ab_im)


def _ssm_kernel(u_ref, bb_ref, cc_ref, are_ref, aim_ref, d_ref, y_ref, x_scr, carry_scr, *, t_chunk):
    c = pl.program_id(0)
    s = pl.program_id(1)
    hs = HALF_STATES
    n_cg = 2 * hs // LANES
    n_re = hs // LANES
    half_rows = SUBLANES // 2


    @pl.when(c == 0)
    def _():
        carry_scr[s] = jnp.zeros((n_cg, SUBLANES, LANES), F32)

    a_re = [are_ref[0, :, g * LANES:(g + 1) * LANES] for g in range(n_re)]
    a_im = [aim_ref[0, :, g * LANES:(g + 1) * LANES] for g in range(n_re)]
    sub = t_chunk // SCAN_SUBCHUNKS

    def project_in(k):
        part = slice(k * sub * SUBLANES, (k + 1) * sub * SUBLANES)
        u3 = u_ref[k * sub:(k + 1) * sub]
        u_lo = u3.reshape(sub * SUBLANES, SLAB)
        u_hi = pltpu.roll(u3, half_rows, 1).reshape(sub * SUBLANES, SLAB)
        lhs = jnp.concatenate([u_lo, u_hi], axis=1).astype(BF16)
        bu = jnp.dot(lhs, bb_ref[0], preferred_element_type=F32)
        for g in range(n_cg):
            x_scr[g, part, :] = bu[:, g * LANES:(g + 1) * LANES]

    def scan(k, x):
        for t in range(k * sub, (k + 1) * sub):
            idx = slice(t * SUBLANES, (t + 1) * SUBLANES)
            new = [None] * n_cg
            for g in range(n_re):
                x_re, x_im = x[g], x[n_re + g]
                new[g] = a_re[g] * x_re - a_im[g] * x_im + x_scr[g, idx, :]
                new[n_re + g] = a_re[g] * x_im + a_im[g] * x_re + x_scr[n_re + g, idx, :]
            for g in range(n_cg):
                x_scr[g, idx, :] = new[g]
            x = new
        return x

    def project_out(k):
        part = slice(k * sub * SUBLANES, (k + 1) * sub * SUBLANES)
        steps = slice(k * sub, (k + 1) * sub)
        xs = jnp.concatenate([x_scr[g, part, :] for g in range(n_cg)], axis=1).astype(BF16)
        yy = jnp.dot(xs, cc_ref[0], preferred_element_type=F32)
        y_half1 = pltpu.roll(yy[:, SLAB:].reshape(sub, SUBLANES, SLAB), half_rows, 1)
        y3 = yy[:, :SLAB].reshape(sub, SUBLANES, SLAB) + y_half1
        batch_rows = lax.broadcasted_iota(jnp.int32, y3.shape, 1) < half_rows
        y_ref[steps] = jnp.where(batch_rows, y3, 0.0) + d_ref[...] * u_ref[steps]

    x = [carry_scr[s, g] for g in range(n_cg)]
    project_in(0)
    for k in range(SCAN_SUBCHUNKS):
        if k + 1 < SCAN_SUBCHUNKS:
            project_in(k + 1)
        x = scan(k, x)
        if k > 0:
            project_out(k - 1)
    project_out(SCAN_SUBCHUNKS - 1)
    for g in range(n_cg):
        carry_scr[s, g] = x[g]


def _ssm(u, bb, cc, a_re, a_im, d_skip, t_chunk=256):
    s, r, width = u.shape
    assert r == SUBLANES
    n_slab = width // SLAB
    hs = HALF_STATES
    return pl.pallas_call(
        functools.partial(_ssm_kernel, t_chunk=t_chunk),
        grid=(s // t_chunk, n_slab),
        in_specs=[pl.BlockSpec((t_chunk, SUBLANES, SLAB), lambda c, sl: (c, 0, sl)),
                  pl.BlockSpec((1, 2 * SLAB, 2 * hs), lambda c, sl: (sl, 0, 0)),
                  pl.BlockSpec((1, 2 * hs, 2 * SLAB), lambda c, sl: (sl, 0, 0)),
                  pl.BlockSpec((1, SUBLANES, hs), lambda c, sl: (sl, 0, 0)),
                  pl.BlockSpec((1, SUBLANES, hs), lambda c, sl: (sl, 0, 0)),
                  pl.BlockSpec((1, SLAB), lambda c, sl: (0, sl))],
        out_specs=pl.BlockSpec((t_chunk, SUBLANES, SLAB), lambda c, sl: (c, 0, sl)),
        out_shape=jax.ShapeDtypeStruct((s, SUBLANES, width), F32),
        scratch_shapes=[pltpu.VMEM((2 * hs // LANES, SUBLANES * t_chunk, LANES), F32),
                        pltpu.VMEM((n_slab, 2 * hs // LANES, SUBLANES, LANES), F32)],
        compiler_params=_params(("arbitrary", "arbitrary")),
        name="s5_scan",
    )(u, bb, cc, a_re, a_im, d_skip.reshape(1, width))


def _mix_kernel(a0_ref, a1_ref, y_ref, x_ref, mod_ref, wglu_ref, bglu_ref, ga_ref, gs_ref, wo_ref, lng_ref, lnb_ref,
                o_ref):
    aw = a0_ref.shape[1]
    a = _rms_norm_rows(jnp.concatenate([a0_ref[...], a1_ref[...]], axis=0), ga_ref[...])
    y = jax.nn.gelu(y_ref[...])
    z = y * jax.nn.sigmoid(jnp.dot(y.astype(BF16), wglu_ref[...], preferred_element_type=F32) + bglu_ref[...])
    sn = _rms_norm_rows(z, gs_ref[...])
    mix = (jnp.dot(a.astype(BF16), wo_ref[0:aw, :], preferred_element_type=F32)
           + jnp.dot(sn.astype(BF16), wo_ref[aw:, :], preferred_element_type=F32))
    gate = mod_ref[0, 5:6, :]
    o_ref[...] = _residual_layer_norm(x_ref[...], mix, 1.0 + gate, lng_ref[...], lnb_ref[...])


def _mix(attn, y_rows, x2d, mod, w_glu, b_glu, g_attn, g_ssm, w_out, ln_g, ln_b, *, seq):
    n, d = x2d.shape
    _, _, n_steps, blk, aw = attn.shape
    tm = 2 * blk
    sw = y_rows.shape[1] // SUBLANES
    tiles_per_batch = seq // tm
    const = lambda m: (0, 0)

    def attn_spec(e):
        def index(m):
            i = 2 * (m % tiles_per_batch) + e
            folded = i >= n_steps
            return (m // tiles_per_batch, folded.astype(jnp.int32), jnp.where(folded, 2 * n_steps - 1 - i, i), 0, 0)
        return pl.BlockSpec((None, None, None, blk, aw), index)

    return pl.pallas_call(
        _mix_kernel,
        grid=(n // tm,),
        in_specs=[attn_spec(0), attn_spec(1),
                  pl.BlockSpec((tm, sw), lambda m: (m % tiles_per_batch, m // tiles_per_batch)),
                  pl.BlockSpec((tm, d), lambda m: (m, 0)),
                  pl.BlockSpec((1, N_MOD, d), lambda m: (m // tiles_per_batch, 0, 0)),
                  pl.BlockSpec((sw, sw), const),
                  pl.BlockSpec((1, sw), const),
                  pl.BlockSpec((1, aw), const),
                  pl.BlockSpec((1, sw), const),
                  pl.BlockSpec((aw + sw, d), const),
                  pl.BlockSpec((1, d), const),
                  pl.BlockSpec((1, d), const)],
        out_specs=pl.BlockSpec((tm, d), lambda m: (m, 0)),
        out_shape=jax.ShapeDtypeStruct((n, d), F32),
        compiler_params=_params(("parallel",)),
        name="mix",
    )(attn, attn, y_rows, x2d, mod, w_glu, b_glu.reshape(1, sw), g_attn.reshape(1, aw), g_ssm.reshape(1, sw),
      w_out, ln_g.reshape(1, d), ln_b.reshape(1, d))


def kernel(x, c, w_ada, b_ada, ffn1_w_gate, ffn1_w_up, ffn1_w_down, ln1_g, ln1_b, w_in, attn_norm_g, ssm_lambda_re, ssm_lambda_im, ssm_log_dt, ssm_b_re, ssm_b_im, ssm_c_re, ssm_c_im, ssm_d, ssm_w_glu, ssm_b_glu, ssm_norm_g, w_out, ln2_g, ln2_b, ffn2_w_gate, ffn2_w_up, ffn2_w_down, ln3_g, ln3_b):
    b, s, d = x.shape
    assert w_ada.shape[0] == DEPTH
    slopes = 2.0 ** (-(8.0 / N_HEADS) * jnp.arange(1, N_HEADS + 1, dtype=F32))
    c_pad = jnp.pad(c, ((0, SUBLANES - b), (0, 0)))
    x2d = x.reshape(b * s, d)
    for l in range(DEPTH):
        mod = _adaln(c_pad, w_ada[l], b_ada[l])[:b].reshape(b, N_MOD, d)

        later_weights = (ffn2_w_gate[l], ffn2_w_up[l], ffn2_w_down[l], w_in[l], w_out[l], ssm_w_glu[l])
        x2d, (w2_gate, w2_up, w2_down, w_in_bf, w_out_bf, w_glu_bf) = _ffn(
            x2d, mod, ffn1_w_gate[l].astype(BF16), ffn1_w_up[l].astype(BF16), ffn1_w_down[l].astype(BF16),
            ln1_g[l], ln1_b[l], mod_base=0, seq=s, cast_jobs=later_weights)

        q, k, v, u = _inproj(x2d, mod, w_in_bf, seq=s)
        aw = q.shape[1]
        attn = _attention(q.reshape(b, s, aw), k.reshape(b, s, aw), v.reshape(b, s, aw), slopes)

        bb, cc, a_re, a_im = _ssm_tables(ssm_lambda_re[l], ssm_lambda_im[l], ssm_log_dt[l],
                                         ssm_b_re[l], ssm_b_im[l], ssm_c_re[l], ssm_c_im[l])
        sw = u.shape[1] // SUBLANES
        y = _ssm(u.reshape(s, SUBLANES, sw), bb, cc, a_re, a_im, ssm_d[l])

        x2d = _mix(attn, y.reshape(s, SUBLANES * sw), x2d, mod, w_glu_bf, ssm_b_glu[l],
                   attn_norm_g[l], ssm_norm_g[l], w_out_bf, ln2_g[l], ln2_b[l], seq=s)

        x2d, _ = _ffn(x2d, mod, w2_gate, w2_up, w2_down, ln3_g[l], ln3_b[l], mod_base=6, seq=s)
    return x2d.reshape(b, s, d)
```

```python
import functools
import math

import jax
import jax.numpy as jnp
from jax import lax
from jax.experimental import pallas as pl
from jax.experimental.pallas import tpu as pltpu

F32 = jnp.float32
BF16 = jnp.bfloat16

N_HEADS = 8
HEAD_DIM = 128
MOBA_BLOCK = 256
MOBA_TOPK = 3
SSM_GROUP = 16
SSM_STATE = 64
N_MOD = 9
LN_EPS = 1e-5
NEG_INF = -1e30
LOG2E = math.log2(math.e)
DEPTH = 1
ALPHA = (2.0 * DEPTH) ** 0.25
Q_PRESCALE = HEAD_DIM ** -0.5 * LOG2E

LANES = 128
SUBLANES = 8
VMEM_LIMIT_BYTES = 56 * 1024 * 1024

SLAB = LANES
HALF_STATES = 256
SCAN_SUBCHUNKS = 4


def _params(sem):
    return pltpu.CompilerParams(dimension_semantics=sem, vmem_limit_bytes=VMEM_LIMIT_BYTES)


def _residual_layer_norm(x, y, coef, g, b):
    r = x + (coef * (1.0 / ALPHA)) * y
    mu = jnp.mean(r, axis=-1, keepdims=True)
    d = r - mu
    var = jnp.mean(d * d, axis=-1, keepdims=True)
    return d * lax.rsqrt(var + LN_EPS / (ALPHA * ALPHA)) * g + b


def _rms_norm_rows(y, g):
    return y * lax.rsqrt(jnp.mean(y * y, axis=-1, keepdims=True) + LN_EPS) * g


def _adaln_kernel(c_ref, w_ref, b_ref, o_ref):
    c = c_ref[...]
    c_act = (c * jax.nn.sigmoid(c)).astype(BF16)
    o_ref[...] = jnp.dot(c_act, w_ref[...].astype(BF16), preferred_element_type=F32) + b_ref[...]


def _adaln(c_pad, w_ada, b_ada, tn=1024):
    rows, d = c_pad.shape
    n = w_ada.shape[1]
    return pl.pallas_call(
        _adaln_kernel,
        grid=(n // tn,),
        in_specs=[pl.BlockSpec((rows, d), lambda j: (0, 0)),
                  pl.BlockSpec((d, tn), lambda j: (0, j)),
                  pl.BlockSpec((1, tn), lambda j: (0, j))],
        out_specs=pl.BlockSpec((rows, tn), lambda j: (0, j)),
        out_shape=jax.ShapeDtypeStruct((rows, n), F32),
        compiler_params=_params(("arbitrary",)),
        name="adaln",
    )(c_pad, w_ada, b_ada.reshape(1, n))


def _cast_side_job(cast_in, cast_out):
    for src, dst in zip(cast_in, cast_out):
        dst[...] = src[...].astype(BF16)


def _ffn_kernel(*refs, mod_base, n_cast, odd_tiles):
    x_ref, mod_ref, wga_ref, wua_ref, wda_ref, wgb_ref, wub_ref, wdb_ref, lng_ref, lnb_ref = refs[:10]
    cast_in = refs[10:10 + n_cast]
    o_ref = refs[10 + n_cast]
    cast_out = refs[11 + n_cast:11 + 2 * n_cast]
    u_scr, acc_scr = refs[11 + 2 * n_cast:]
    f = pl.program_id(1)
    last = pl.num_programs(1) - 1

    @pl.when(f == 0)
    def _():
        shift = mod_ref[0, mod_base:mod_base + 1, :]
        scale = mod_ref[0, mod_base + 1:mod_base + 2, :]
        u_scr[...] = (x_ref[...] * (1.0 + scale) + shift).astype(BF16)
        acc_scr[...] = jnp.zeros_like(acc_scr)

    def swiglu(wg_ref, wu_ref, wd_ref):
        u = u_scr[...]
        g = jnp.dot(u, wg_ref[...], preferred_element_type=F32)
        up = jnp.dot(u, wu_ref[...], preferred_element_type=F32)
        h = (g * jax.nn.sigmoid(g)) * up
        return jnp.dot(h.astype(BF16), wd_ref[...], preferred_element_type=F32)

    def both_tiles():
        y = swiglu(wga_ref, wua_ref, wda_ref)
        _cast_side_job(cast_in, cast_out)
        return y + swiglu(wgb_ref, wub_ref, wdb_ref)

    def finish(y):
        gate = mod_ref[0, mod_base + 2:mod_base + 3, :]
        o_ref[...] = _residual_layer_norm(x_ref[...], acc_scr[...] + y, 0.5 * (1.0 + gate), lng_ref[...], lnb_ref[...])

    @pl.when(f < last)
    def _():
        acc_scr[...] += both_tiles()

    @pl.when(f == last)
    def _():
        if odd_tiles:
            _cast_side_job(cast_in, cast_out)
            finish(swiglu(wga_ref, wua_ref, wda_ref))
        else:
            finish(both_tiles())


def _cast_job_spec(w, n_m, n_f):
    rows, cols = w.shape
    bf16_rows = 2 * SUBLANES
    if rows % n_m == 0 and cols % n_f == 0 and (rows // n_m) % bf16_rows == 0 and (cols // n_f) % LANES == 0:
        return pl.BlockSpec((rows // n_m, cols // n_f), lambda m, f: (m, f))
    if rows % (n_m * n_f) == 0 and (rows // (n_m * n_f)) % bf16_rows == 0:
        return pl.BlockSpec((rows // (n_m * n_f), cols), lambda m, f: (m * n_f + f, 0))
    assert rows % n_m == 0 and (rows // n_m) % bf16_rows == 0, w.shape
    return pl.BlockSpec((rows // n_m, cols), lambda m, f: (m, 0))


def _ffn(x2d, mod, wg, wu, wd, ln_g, ln_b, *, mod_base, seq, cast_jobs=(), tm=512, tf=512):
    n, d = x2d.shape
    dff = wg.shape[1]
    tiles_per_batch = seq // tm
    n_tiles = dff // tf
    grid = (n // tm, (n_tiles + 1) // 2)
    tile_a = lambda f: 2 * f
    tile_b = lambda f: jnp.minimum(2 * f + 1, n_tiles - 1)
    cast_specs = [_cast_job_spec(w, *grid) for w in cast_jobs]
    out = pl.pallas_call(
        functools.partial(_ffn_kernel, mod_base=mod_base, n_cast=len(cast_jobs), odd_tiles=n_tiles % 2 == 1),
        grid=grid,
        in_specs=[pl.BlockSpec((tm, d), lambda m, f: (m, 0)),
                  pl.BlockSpec((1, N_MOD, d), lambda m, f: (m // tiles_per_batch, 0, 0)),
                  pl.BlockSpec((d, tf), lambda m, f: (0, tile_a(f))),
                  pl.BlockSpec((d, tf), lambda m, f: (0, tile_a(f))),
                  pl.BlockSpec((tf, d), lambda m, f: (tile_a(f), 0)),
                  pl.BlockSpec((d, tf), lambda m, f: (0, tile_b(f))),
                  pl.BlockSpec((d, tf), lambda m, f: (0, tile_b(f))),
                  pl.BlockSpec((tf, d), lambda m, f: (tile_b(f), 0)),
                  pl.BlockSpec((1, d), lambda m, f: (0, 0)),
                  pl.BlockSpec((1, d), lambda m, f: (0, 0))] + cast_specs,
        out_specs=[pl.BlockSpec((tm, d), lambda m, f: (m, 0))] + cast_specs,
        out_shape=[jax.ShapeDtypeStruct((n, d), F32)] + [jax.ShapeDtypeStruct(w.shape, BF16) for w in cast_jobs],
        scratch_shapes=[pltpu.VMEM((tm, d), BF16), pltpu.VMEM((tm, d), F32)],
        compiler_params=_params(("arbitrary", "arbitrary")),
        name="ffn",
    )(x2d, mod, wg, wu, wd, wg, wu, wd, ln_g.reshape(1, d), ln_b.reshape(1, d), *cast_jobs)
    return out[0], out[1:]


def _inproj_kernel(*refs, n_cast):
    x_ref, mod_ref, w_ref, _ = refs[:4]
    cast_in = refs[4:4 + n_cast]
    q_ref, k_ref, v_ref, u_ref = refs[4 + n_cast:8 + n_cast]
    cast_out = refs[8 + n_cast:8 + 2 * n_cast]
    xm_scr = refs[8 + 2 * n_cast]
    j = pl.program_id(1)

    @pl.when(j == 0)
    def _():
        shift = mod_ref[0, 3:4, :]
        scale = mod_ref[0, 4:5, :]
        xm_scr[...] = (x_ref[...] * (1.0 + scale) + shift).astype(BF16)

    h = jnp.dot(xm_scr[...], w_ref[...], preferred_element_type=F32)
    _cast_side_job(cast_in, cast_out)
    width = q_ref.shape[1]

    @pl.when(j == 0)
    def _():
        q_ref[...] = (h[:, :width] * Q_PRESCALE).astype(BF16)
        k_ref[...] = h[:, width:].astype(BF16)

    @pl.when(j == 1)
    def _():
        v_ref[...] = h[:, :width].astype(BF16)
        u_ref[...] = h[:, width:]


def _inproj(x2d, mod, w_in, *, seq, cast_jobs=(), tm=512):
    n, d = x2d.shape
    width = w_in.shape[1] // 4
    tiles_per_batch = seq // tm
    assert n // seq <= SUBLANES // 2
    grid = (n // tm, 2)
    out_spec = pl.BlockSpec((tm, width), lambda m, j: (m, 0))
    u_spec = pl.BlockSpec((tm, width), lambda m, j: (m % tiles_per_batch, m // tiles_per_batch))
    u_zeros = jnp.zeros((seq, SUBLANES * width), F32)
    cast_specs = [_cast_job_spec(w, *grid) for w in cast_jobs]
    out = pl.pallas_call(
        functools.partial(_inproj_kernel, n_cast=len(cast_jobs)),
        grid=grid,
        in_specs=[pl.BlockSpec((tm, d), lambda m, j: (m, 0)),
                  pl.BlockSpec((1, N_MOD, d), lambda m, j: (m // tiles_per_batch, 0, 0)),
                  pl.BlockSpec((d, 2 * width), lambda m, j: (0, j)),
                  pl.BlockSpec(memory_space=pl.ANY)] + cast_specs,
        out_specs=[out_spec, out_spec, out_spec, u_spec] + cast_specs,
        out_shape=([jax.ShapeDtypeStruct((n, width), BF16)] * 3 + [jax.ShapeDtypeStruct((seq, SUBLANES * width), F32)]
                   + [jax.ShapeDtypeStruct(w.shape, BF16) for w in cast_jobs]),
        input_output_aliases={3: 3},
        scratch_shapes=[pltpu.VMEM((tm, d), BF16)],
        compiler_params=_params(("arbitrary", "arbitrary")),
        name="inproj",
    )(x2d, mod, w_in, u_zeros, *cast_jobs)
    return out[:4], out[4:]


def _dot_nt(a, b):
    return lax.dot_general(a, b, (((1,), (1,)), ((), ())), preferred_element_type=F32)


def _attn_kernel(slopes_ref, qa_ref, qb_ref, k_ref, v_ref, o_ref, kmean_scr, vt_scr, base_scr, q_scr, pk_scr, m_scr,
                 acc_scr, sa_scr, sb_scr, pa_scr, pb_scr, *, n_blocks, hg):
    g = pl.program_id(1)
    p = pl.program_id(2)
    blk = MOBA_BLOCK
    dh = HEAD_DIM
    slope = [slopes_ref[g * hg + h] for h in range(hg)]
    hcols = [slice(h * dh, (h + 1) * dh) for h in range(hg)]
    q_refs = (qa_ref, qb_ref)
    tile_blk = (p, n_blocks - 1 - p)

    key_id = lax.broadcasted_iota(jnp.int32, (blk, blk), 0)
    qry_id = lax.broadcasted_iota(jnp.int32, (blk, blk), 1)
    rel = (qry_id - key_id).astype(F32)

    @pl.when(p == 0)
    def _():
        row_id = lax.broadcasted_iota(jnp.int32, (vt_scr.shape[1] - dh, vt_scr.shape[2]), 0)
        ones_row = jnp.where(row_id == 0, 1.0, 0.0).astype(BF16)
        for h in range(hg):
            base_scr[h] = (-LOG2E * slope[h]) * rel
            vt_scr[h, dh:, :] = ones_row
            for j in range(n_blocks):
                rows = slice(j * blk, (j + 1) * blk)
                kmean_scr[h, j:j + 1, :] = jnp.mean(k_ref[0, rows, hcols[h]].astype(F32), axis=0, keepdims=True)
                vt_scr[h, 0:dh, rows] = v_ref[0, rows, hcols[h]].astype(F32).T.astype(BF16)

    def item(n):
        tile = (n >= p).astype(jnp.int32)
        return tile, n - p * tile, jnp.where(tile == 0, tile_blk[0], tile_blk[1])

    def scores_into(n, s_scr):
        tile, j, _ = item(n)
        keys = pl.ds(pl.multiple_of(j * blk, blk), blk)
        for h in range(hg):
            s_scr[h] = _dot_nt(k_ref[0, keys, hcols[h]], q_scr[tile, h])

    def softmax_stage(n, s_scr, p_scr):
        tile, j, i_blk = item(n)
        j_f = j.astype(F32)
        i_f = i_blk.astype(F32)
        alphas = []
        for h in range(hg):
            selected = ((pk_scr[tile, h, 0:1, :] == j_f) | (pk_scr[tile, h, 1:2, :] == j_f)
                        | (pk_scr[tile, h, 2:3, :] == j_f))
            row_bias = jnp.where(selected, (j_f - i_f) * (float(blk) * LOG2E * slope[h]), NEG_INF)
            x = s_scr[h] + base_scr[h]
            m_old = m_scr[tile, h]
            m_new = jnp.maximum(m_old, jnp.max(x, axis=0, keepdims=True) + row_bias)
            p_scr[h] = jnp.exp2(x - (m_new - row_bias)).astype(BF16)
            m_scr[tile, h] = m_new
            alphas.append(jnp.exp2(m_old - m_new))
        return tuple(alphas)

    def accumulate(n, p_scr, alphas):
        tile, j, _ = item(n)
        keys = pl.ds(pl.multiple_of(j * blk, blk), blk)
        pv = [jnp.dot(vt_scr[h, :, keys], p_scr[h], preferred_element_type=F32) for h in range(hg)]
        for h in range(hg):
            acc_scr[tile, h] = alphas[h] * acc_scr[tile, h] + pv[h]

    for t in range(2):
        for h in range(hg):
            q_scr[t, h] = q_refs[t][0, 0, :, hcols[h]]
    scores_into(0, sa_scr)
    own = [pl.ds(pl.multiple_of(tile_blk[t] * blk, blk), blk) for t in range(2)]
    gates = [[_dot_nt(kmean_scr[h].astype(BF16), q_scr[t, h]) for h in range(hg)] for t in range(2)]
    s_own = [[_dot_nt(k_ref[0, own[t], hcols[h]], q_scr[t, h]) for h in range(hg)] for t in range(2)]
    p_own = [[None] * hg for _ in range(2)]
    for t in range(2):
        i_f = tile_blk[t].astype(F32)
        for h in range(hg):
            blk_id = lax.broadcasted_iota(jnp.int32, gates[t][h].shape, 0).astype(F32)
            work = jnp.where(blk_id < i_f, gates[t][h], NEG_INF)
            for r in range(MOBA_TOPK):
                mx = jnp.max(work, axis=0, keepdims=True)
                idx = jnp.min(jnp.where(work == mx, blk_id, float(n_blocks)), axis=0, keepdims=True)
                work = jnp.where(blk_id == idx, -jnp.inf, work)
                pk_scr[t, h, r:r + 1, :] = jnp.where(idx < i_f, idx, -1.0)
            s = jnp.where(rel >= 0.0, s_own[t][h] + base_scr[h], NEG_INF)
            m = jnp.max(s, axis=0, keepdims=True)
            m_scr[t, h] = m
            p_own[t][h] = jnp.exp2(s - m).astype(BF16)
    for t in range(2):
        pv_own = [jnp.dot(vt_scr[h, :, own[t]], p_own[t][h], preferred_element_type=F32) for h in range(hg)]
        for h in range(hg):
            acc_scr[t, h] = pv_own[h]

    s_slots = (sa_scr, sb_scr)
    p_slots = (pa_scr, pb_scr)
    a_prev = None
    for n in range(n_blocks - 1):
        if n + 1 < n_blocks - 1:
            scores_into(n + 1, s_slots[(n + 1) % 2])
        a_cur = softmax_stage(n, s_slots[n % 2], p_slots[n % 2])
        if n > 0:
            accumulate(n - 1, p_slots[(n - 1) % 2], a_prev)
        a_prev = a_cur
    accumulate(n_blocks - 2, p_slots[(n_blocks - 2) % 2], a_prev)
    for t in range(2):
        for h in range(hg):
            acc = acc_scr[t, h]
            o_ref[0, t, 0, :, hcols[h]] = (acc[0:dh] / acc[dh:dh + 1]).T


def _attention(q, k, v, slopes, hg=4):
    b, s, width = q.shape
    n_heads = width // HEAD_DIM
    n_blocks = s // MOBA_BLOCK
    assert n_blocks % 2 == 0 and n_heads % hg == 0
    n_steps = n_blocks // 2
    gw = hg * HEAD_DIM
    ones_rows = 16
    kv_spec = pl.BlockSpec((1, s, gw), lambda bi, g, p: (bi, 0, g))
    slot = lambda dims, dt: pltpu.VMEM(dims, dt)
    return pl.pallas_call(
        functools.partial(_attn_kernel, n_blocks=n_blocks, hg=hg),
        grid=(b, n_heads // hg, n_steps),
        in_specs=[pl.BlockSpec(memory_space=pltpu.SMEM),
                  pl.BlockSpec((1, 1, MOBA_BLOCK, gw), lambda bi, g, p: (bi, p, 0, g)),
                  pl.BlockSpec((1, 1, MOBA_BLOCK, gw), lambda bi, g, p: (bi, n_blocks - 1 - p, 0, g)),
                  kv_spec, kv_spec],
        out_specs=pl.BlockSpec((1, 2, 1, MOBA_BLOCK, gw), lambda bi, g, p: (bi, 0, p, 0, g)),
        out_shape=jax.ShapeDtypeStruct((b, 2, n_steps, MOBA_BLOCK, width), F32),
        scratch_shapes=[slot((hg, n_blocks, HEAD_DIM), F32),
                        slot((hg, HEAD_DIM + ones_rows, s), BF16),
                        slot((hg, MOBA_BLOCK, MOBA_BLOCK), F32),
                        slot((2, hg, MOBA_BLOCK, HEAD_DIM), BF16),
                        slot((2, hg, SUBLANES, MOBA_BLOCK), F32),
                        slot((2, hg, 1, MOBA_BLOCK), F32),
                        slot((2, hg, HEAD_DIM + ones_rows, MOBA_BLOCK), F32),
                        slot((hg, MOBA_BLOCK, MOBA_BLOCK), F32), slot((hg, MOBA_BLOCK, MOBA_BLOCK), F32),
                        slot((hg, MOBA_BLOCK, MOBA_BLOCK), BF16), slot((hg, MOBA_BLOCK, MOBA_BLOCK), BF16)],
        compiler_params=_params(("parallel", "parallel", "arbitrary")),
        name="moba_attn",
    )(slopes, q.reshape(b, n_blocks, MOBA_BLOCK, width), q.reshape(b, n_blocks, MOBA_BLOCK, width), k, v)


def _ssm_tables(lam_re, lam_im, log_dt, b_re, b_im, c_re, c_im):
    g_total, p = lam_re.shape
    n_slab = g_total * SSM_GROUP // SLAB
    gps = SLAB // SSM_GROUP
    gph = gps // 2
    dt = jnp.exp(log_dt)[:, None]
    decay = jnp.exp(lam_re * dt)
    ab_re = decay * jnp.cos(lam_im * dt)
    ab_im = decay * jnp.sin(lam_im * dt)
    den = lam_re * lam_re + lam_im * lam_im
    coef_re = ((ab_re - 1.0) * lam_re + ab_im * lam_im) / den
    coef_im = (ab_im * lam_re - (ab_re - 1.0) * lam_im) / den
    bb_re = coef_re[..., None] * b_re - coef_im[..., None] * b_im
    bb_im = coef_re[..., None] * b_im + coef_im[..., None] * b_re
    eye = jnp.eye(gps, dtype=F32)

    bbri = jnp.stack([bb_re, bb_im], axis=0).reshape(2, n_slab, gps, p, SSM_GROUP)
    bb = jnp.einsum('rsgph,gk->sghkrp', bbri, eye)
    bb = bb.reshape(n_slab, gps, SSM_GROUP, 2, gph, 2, p).transpose(0, 1, 2, 3, 5, 4, 6)
    bb = bb.reshape(n_slab, SLAB, 2, 2 * HALF_STATES).transpose(0, 2, 1, 3).reshape(n_slab, 2 * SLAB, 2 * HALF_STATES)

    ccri = jnp.stack([c_re, -c_im], axis=0).reshape(2, n_slab, gps, SSM_GROUP, p)
    cc = jnp.einsum('rsghp,gk->skrpgh', ccri, eye)
    cc = cc.reshape(n_slab, 2, gph, 2, p, gps, SSM_GROUP).transpose(0, 1, 3, 2, 4, 5, 6)
    cc = cc.reshape(n_slab, 2, 2 * HALF_STATES, SLAB).transpose(0, 2, 1, 3).reshape(n_slab, 2 * HALF_STATES, 2 * SLAB)

    def rows(a):
        a = a.reshape(n_slab, 2, 1, HALF_STATES)
        return jnp.broadcast_to(a, (n_slab, 2, 4, HALF_STATES)).reshape(n_slab, SUBLANES, HALF_STATES)

    return bb.astype(BF16), cc.astype(BF16), rows(ab_re), rows(ab_im)


def _ssm_kernel(u_ref, bb_ref, cc_ref, are_ref, aim_ref, d_ref, y_ref, x_scr, carry_scr, *, t_chunk):
    c = pl.program_id(0)
    s = pl.program_id(1)
    hs = HALF_STATES
    n_cg = 2 * hs // LANES
    n_re = hs // LANES
    half_rows = SUBLANES // 2


    @pl.when(c == 0)
    def _():
        carry_scr[s] = jnp.zeros((n_cg, SUBLANES, LANES), F32)

    a_re = [are_ref[0, :, g * LANES:(g + 1) * LANES] for g in range(n_re)]
    a_im = [aim_ref[0, :, g * LANES:(g + 1) * LANES] for g in range(n_re)]
    sub = t_chunk // SCAN_SUBCHUNKS

    def project_in(k):
        part = slice(k * sub * SUBLANES, (k + 1) * sub * SUBLANES)
        u3 = u_ref[k * sub:(k + 1) * sub]
        u_lo = u3.reshape(sub * SUBLANES, SLAB)
        u_hi = pltpu.roll(u3, half_rows, 1).reshape(sub * SUBLANES, SLAB)
        lhs = jnp.concatenate([u_lo, u_hi], axis=1).astype(BF16)
        bu = jnp.dot(lhs, bb_ref[0], preferred_element_type=F32)
        for g in range(n_cg):
            x_scr[g, part, :] = bu[:, g * LANES:(g + 1) * LANES]

    def scan(k, x):
        for t in range(k * sub, (k + 1) * sub):
            idx = slice(t * SUBLANES, (t + 1) * SUBLANES)
            new = [None] * n_cg
            for g in range(n_re):
                x_re, x_im = x[g], x[n_re + g]
                new[g] = a_re[g] * x_re - a_im[g] * x_im + x_scr[g, idx, :]
                new[n_re + g] = a_re[g] * x_im + a_im[g] * x_re + x_scr[n_re + g, idx, :]
            for g in range(n_cg):
                x_scr[g, idx, :] = new[g]
            x = new
        return x

    def project_out(k):
        part = slice(k * sub * SUBLANES, (k + 1) * sub * SUBLANES)
        steps = slice(k * sub, (k + 1) * sub)
        xs = jnp.concatenate([x_scr[g, part, :] for g in range(n_cg)], axis=1).astype(BF16)
        yy = jnp.dot(xs, cc_ref[0], preferred_element_type=F32)
        y_half1 = pltpu.roll(yy[:, SLAB:].reshape(sub, SUBLANES, SLAB), half_rows, 1)
        y3 = yy[:, :SLAB].reshape(sub, SUBLANES, SLAB) + y_half1
        batch_rows = lax.broadcasted_iota(jnp.int32, y3.shape, 1) < half_rows
        y_ref[steps] = jnp.where(batch_rows, y3, 0.0) + d_ref[...] * u_ref[steps]

    x = [carry_scr[s, g] for g in range(n_cg)]
    project_in(0)
    for k in range(SCAN_SUBCHUNKS):
        if k + 1 < SCAN_SUBCHUNKS:
            project_in(k + 1)
        x = scan(k, x)
        if k > 0:
            project_out(k - 1)
    project_out(SCAN_SUBCHUNKS - 1)
    for g in range(n_cg):
        carry_scr[s, g] = x[g]


def _ssm(u, bb, cc, a_re, a_im, d_skip, t_chunk=256):
    s, r, width = u.shape
    assert r == SUBLANES
    n_slab = width // SLAB
    hs = HALF_STATES
    return pl.pallas_call(
        functools.partial(_ssm_kernel, t_chunk=t_chunk),
        grid=(s // t_chunk, n_slab),
        in_specs=[pl.BlockSpec((t_chunk, SUBLANES, SLAB), lambda c, sl: (c, 0, sl)),
                  pl.BlockSpec((1, 2 * SLAB, 2 * hs), lambda c, sl: (sl, 0, 0)),
                  pl.BlockSpec((1, 2 * hs, 2 * SLAB), lambda c, sl: (sl, 0, 0)),
                  pl.BlockSpec((1, SUBLANES, hs), lambda c, sl: (sl, 0, 0)),
                  pl.BlockSpec((1, SUBLANES, hs), lambda c, sl: (sl, 0, 0)),
                  pl.BlockSpec((1, SLAB), lambda c, sl: (0, sl))],
        out_specs=pl.BlockSpec((t_chunk, SUBLANES, SLAB), lambda c, sl: (c, 0, sl)),
        out_shape=jax.ShapeDtypeStruct((s, SUBLANES, width), F32),
        scratch_shapes=[pltpu.VMEM((2 * hs // LANES, SUBLANES * t_chunk, LANES), F32),
                        pltpu.VMEM((n_slab, 2 * hs // LANES, SUBLANES, LANES), F32)],
        compiler_params=_params(("arbitrary", "arbitrary")),
        name="s5_scan",
    )(u, bb, cc, a_re, a_im, d_skip.reshape(1, width))


def _mix_kernel(a0_ref, a1_ref, y_ref, x_ref, mod_ref, wglu_ref, bglu_ref, ga_ref, gs_ref, wo_ref, lng_ref, lnb_ref,
                o_ref):
    aw = a0_ref.shape[1]
    a = _rms_norm_rows(jnp.concatenate([a0_ref[...], a1_ref[...]], axis=0), ga_ref[...])
    y = jax.nn.gelu(y_ref[...])
    z = y * jax.nn.sigmoid(jnp.dot(y.astype(BF16), wglu_ref[...], preferred_element_type=F32) + bglu_ref[...])
    sn = _rms_norm_rows(z, gs_ref[...])
    mix = (jnp.dot(a.astype(BF16), wo_ref[0:aw, :], preferred_element_type=F32)
           + jnp.dot(sn.astype(BF16), wo_ref[aw:, :], preferred_element_type=F32))
    gate = mod_ref[0, 5:6, :]
    o_ref[...] = _residual_layer_norm(x_ref[...], mix, 1.0 + gate, lng_ref[...], lnb_ref[...])


def _mix(attn, y_rows, x2d, mod, w_glu, b_glu, g_attn, g_ssm, w_out, ln_g, ln_b, *, seq):
    n, d = x2d.shape
    _, _, n_steps, blk, aw = attn.shape
    tm = 2 * blk
    sw = y_rows.shape[1] // SUBLANES
    tiles_per_batch = seq // tm
    const = lambda m: (0, 0)

    def attn_spec(e):
        def index(m):
            i = 2 * (m % tiles_per_batch) + e
            folded = i >= n_steps
            return (m // tiles_per_batch, folded.astype(jnp.int32), jnp.where(folded, 2 * n_steps - 1 - i, i), 0, 0)
        return pl.BlockSpec((None, None, None, blk, aw), index)

    return pl.pallas_call(
        _mix_kernel,
        grid=(n // tm,),
        in_specs=[attn_spec(0), attn_spec(1),
                  pl.BlockSpec((tm, sw), lambda m: (m % tiles_per_batch, m // tiles_per_batch)),
                  pl.BlockSpec((tm, d), lambda m: (m, 0)),
                  pl.BlockSpec((1, N_MOD, d), lambda m: (m // tiles_per_batch, 0, 0)),
                  pl.BlockSpec((sw, sw), const),
                  pl.BlockSpec((1, sw), const),
                  pl.BlockSpec((1, aw), const),
                  pl.BlockSpec((1, sw), const),
                  pl.BlockSpec((aw + sw, d), const),
                  pl.BlockSpec((1, d), const),
                  pl.BlockSpec((1, d), const)],
        out_specs=pl.BlockSpec((tm, d), lambda m: (m, 0)),
        out_shape=jax.ShapeDtypeStruct((n, d), F32),
        compiler_params=_params(("parallel",)),
        name="mix",
    )(attn, attn, y_rows, x2d, mod, w_glu, b_glu.reshape(1, sw), g_attn.reshape(1, aw), g_ssm.reshape(1, sw),
      w_out, ln_g.reshape(1, d), ln_b.reshape(1, d))


def kernel(x, c, w_ada, b_ada, ffn1_w_gate, ffn1_w_up, ffn1_w_down, ln1_g, ln1_b, w_in, attn_norm_g, ssm_lambda_re, ssm_lambda_im, ssm_log_dt, ssm_b_re, ssm_b_im, ssm_c_re, ssm_c_im, ssm_d, ssm_w_glu, ssm_b_glu, ssm_norm_g, w_out, ln2_g, ln2_b, ffn2_w_gate, ffn2_w_up, ffn2_w_down, ln3_g, ln3_b):
    b, s, d = x.shape
    assert w_ada.shape[0] == DEPTH
    slopes = 2.0 ** (-(8.0 / N_HEADS) * jnp.arange(1, N_HEADS + 1, dtype=F32))
    c_pad = jnp.pad(c, ((0, SUBLANES - b), (0, 0)))
    x2d = x.reshape(b * s, d)
    for l in range(DEPTH):
        mod = _adaln(c_pad, w_ada[l], b_ada[l])[:b].reshape(b, N_MOD, d)

        x2d, (w_in_bf,) = _ffn(
            x2d, mod, ffn1_w_gate[l].astype(BF16), ffn1_w_up[l].astype(BF16), ffn1_w_down[l].astype(BF16),
            ln1_g[l], ln1_b[l], mod_base=0, seq=s, cast_jobs=(w_in[l],))

        later_weights = (ffn2_w_gate[l], ffn2_w_up[l], ffn2_w_down[l], w_out[l], ssm_w_glu[l])
        (q, k, v, u), (w2_gate, w2_up, w2_down, w_out_bf, w_glu_bf) = _inproj(
            x2d, mod, w_in_bf, seq=s, cast_jobs=later_weights)
        aw = q.shape[1]
        attn = _attention(q.reshape(b, s, aw), k.reshape(b, s, aw), v.reshape(b, s, aw), slopes)

        bb, cc, a_re, a_im = _ssm_tables(ssm_lambda_re[l], ssm_lambda_im[l], ssm_log_dt[l],
                                         ssm_b_re[l], ssm_b_im[l], ssm_c_re[l], ssm_c_im[l])
        sw = u.shape[1] // SUBLANES
        y = _ssm(u.reshape(s, SUBLANES, sw), bb, cc, a_re, a_im, ssm_d[l])

        x2d = _mix(attn, y.reshape(s, SUBLANES * sw), x2d, mod, w_glu_bf, ssm_b_glu[l],
                   attn_norm_g[l], ssm_norm_g[l], w_out_bf, ln2_g[l], ln2_b[l], seq=s)

        x2d, _ = _ffn(x2d, mod, w2_gate, w2_up, w2_down, ln3_g[l], ln3_b[l], mod_base=6, seq=s)
    return x2d.reshape(b, s, d)
```

```python
import functools
import math

import jax
import jax.numpy as jnp
from jax import lax
from jax.experimental import pallas as pl
from jax.experimental.pallas import tpu as pltpu

F32 = jnp.float32
BF16 = jnp.bfloat16

N_HEADS = 8
HEAD_DIM = 128
MOBA_BLOCK = 256
MOBA_TOPK = 3
SSM_GROUP = 16
SSM_STATE = 64
N_MOD = 9
LN_EPS = 1e-5
NEG_INF = -1e30
LOG2E = math.log2(math.e)
DEPTH = 1
ALPHA = (2.0 * DEPTH) ** 0.25
Q_PRESCALE = HEAD_DIM ** -0.5 * LOG2E

LANES = 128
SUBLANES = 8
VMEM_LIMIT_BYTES = 56 * 1024 * 1024

SLAB = LANES
HALF_STATES = 256
SCAN_SUBCHUNKS = 4


def _params(sem):
    return pltpu.CompilerParams(dimension_semantics=sem, vmem_limit_bytes=VMEM_LIMIT_BYTES)


def _residual_layer_norm(x, y, coef, g, b):
    r = x + (coef * (1.0 / ALPHA)) * y
    mu = jnp.mean(r, axis=-1, keepdims=True)
    d = r - mu
    var = jnp.mean(d * d, axis=-1, keepdims=True)
    return d * lax.rsqrt(var + LN_EPS / (ALPHA * ALPHA)) * g + b


def _rms_norm_rows(y, g):
    return y * lax.rsqrt(jnp.mean(y * y, axis=-1, keepdims=True) + LN_EPS) * g


def _adaln_kernel(c_ref, w_ref, b_ref, o_ref):
    c = c_ref[...]
    c_act = (c * jax.nn.sigmoid(c)).astype(BF16)
    o_ref[...] = jnp.dot(c_act, w_ref[...].astype(BF16), preferred_element_type=F32) + b_ref[...]


def _adaln(c_pad, w_ada, b_ada, tn=1024):
    rows, d = c_pad.shape
    n = w_ada.shape[1]
    return pl.pallas_call(
        _adaln_kernel,
        grid=(n // tn,),
        in_specs=[pl.BlockSpec((rows, d), lambda j: (0, 0)),
                  pl.BlockSpec((d, tn), lambda j: (0, j)),
                  pl.BlockSpec((1, tn), lambda j: (0, j))],
        out_specs=pl.BlockSpec((rows, tn), lambda j: (0, j)),
        out_shape=jax.ShapeDtypeStruct((rows, n), F32),
        compiler_params=_params(("arbitrary",)),
        name="adaln",
    )(c_pad, w_ada, b_ada.reshape(1, n))


def _cast_side_job(cast_in, cast_out):
    for src, dst in zip(cast_in, cast_out):
        dst[...] = src[...].astype(BF16)


def _ffn_kernel(*refs, mod_base, n_cast, odd_tiles):
    x_ref, mod_ref, wga_ref, wua_ref, wda_ref, wgb_ref, wub_ref, wdb_ref, lng_ref, lnb_ref = refs[:10]
    cast_in = refs[10:10 + n_cast]
    o_ref = refs[10 + n_cast]
    cast_out = refs[11 + n_cast:11 + 2 * n_cast]
    u_scr, acc_scr = refs[11 + 2 * n_cast:]
    f = pl.program_id(1)
    last = pl.num_programs(1) - 1

    @pl.when(f == 0)
    def _():
        shift = mod_ref[0, mod_base:mod_base + 1, :]
        scale = mod_ref[0, mod_base + 1:mod_base + 2, :]
        u_scr[...] = (x_ref[...] * (1.0 + scale) + shift).astype(BF16)
        acc_scr[...] = jnp.zeros_like(acc_scr)

    def swiglu(wg_ref, wu_ref, wd_ref):
        u = u_scr[...]
        g = jnp.dot(u, wg_ref[...], preferred_element_type=F32)
        up = jnp.dot(u, wu_ref[...], preferred_element_type=F32)
        h = (g * jax.nn.sigmoid(g)) * up
        return jnp.dot(h.astype(BF16), wd_ref[...], preferred_element_type=F32)

    def both_tiles():
        y = swiglu(wga_ref, wua_ref, wda_ref)
        _cast_side_job(cast_in, cast_out)
        return y + swiglu(wgb_ref, wub_ref, wdb_ref)

    def finish(y):
        gate = mod_ref[0, mod_base + 2:mod_base + 3, :]
        o_ref[...] = _residual_layer_norm(x_ref[...], acc_scr[...] + y, 0.5 * (1.0 + gate), lng_ref[...], lnb_ref[...])

    @pl.when(f < last)
    def _():
        acc_scr[...] += both_tiles()

    @pl.when(f == last)
    def _():
        if odd_tiles:
            _cast_side_job(cast_in, cast_out)
            finish(swiglu(wga_ref, wua_ref, wda_ref))
        else:
            finish(both_tiles())


def _cast_job_spec(w, grid):
    rows, cols = w.shape
    bf16_rows = 2 * SUBLANES
    steps = math.prod(grid)
    n_parts = max(k for k in range(1, steps + 1)
                  if steps % k == 0 and rows % k == 0 and (rows // k) % bf16_rows == 0)
    visits = steps // n_parts

    def index(*ids):
        step = 0
        for i, extent in zip(ids, grid):
            step = step * extent + i
        return step // visits, 0

    return pl.BlockSpec((rows // n_parts, cols), index)


def _ffn(x2d, mod, wg, wu, wd, ln_g, ln_b, *, mod_base, seq, cast_jobs=(), tm=512, tf=512):
    n, d = x2d.shape
    dff = wg.shape[1]
    tiles_per_batch = seq // tm
    n_tiles = dff // tf
    grid = (n // tm, (n_tiles + 1) // 2)
    tile_a = lambda f: 2 * f
    tile_b = lambda f: jnp.minimum(2 * f + 1, n_tiles - 1)
    cast_specs = [_cast_job_spec(w, grid) for w in cast_jobs]
    out = pl.pallas_call(
        functools.partial(_ffn_kernel, mod_base=mod_base, n_cast=len(cast_jobs), odd_tiles=n_tiles % 2 == 1),
        grid=grid,
        in_specs=[pl.BlockSpec((tm, d), lambda m, f: (m, 0)),
                  pl.BlockSpec((1, N_MOD, d), lambda m, f: (m // tiles_per_batch, 0, 0)),
                  pl.BlockSpec((d, tf), lambda m, f: (0, tile_a(f))),
                  pl.BlockSpec((d, tf), lambda m, f: (0, tile_a(f))),
                  pl.BlockSpec((tf, d), lambda m, f: (tile_a(f), 0)),
                  pl.BlockSpec((d, tf), lambda m, f: (0, tile_b(f))),
                  pl.BlockSpec((d, tf), lambda m, f: (0, tile_b(f))),
                  pl.BlockSpec((tf, d), lambda m, f: (tile_b(f), 0)),
                  pl.BlockSpec((1, d), lambda m, f: (0, 0)),
                  pl.BlockSpec((1, d), lambda m, f: (0, 0))] + cast_specs,
        out_specs=[pl.BlockSpec((tm, d), lambda m, f: (m, 0))] + cast_specs,
        out_shape=[jax.ShapeDtypeStruct((n, d), F32)] + [jax.ShapeDtypeStruct(w.shape, BF16) for w in cast_jobs],
        scratch_shapes=[pltpu.VMEM((tm, d), BF16), pltpu.VMEM((tm, d), F32)],
        compiler_params=_params(("arbitrary", "arbitrary")),
        name="ffn",
    )(x2d, mod, wg, wu, wd, wg, wu, wd, ln_g.reshape(1, d), ln_b.reshape(1, d), *cast_jobs)
    return out[0], out[1:]


def _inproj_kernel(*refs, n_cast):
    x_ref, mod_ref, w_ref = refs[:3]
    cast_in = refs[3:3 + n_cast]
    q_ref, k_ref, v_ref, u_ref = refs[3 + n_cast:7 + n_cast]
    cast_out = refs[7 + n_cast:7 + 2 * n_cast]
    xm_scr = refs[7 + 2 * n_cast]
    j = pl.program_id(1)

    @pl.when(j == 0)
    def _():
        shift = mod_ref[0, 3:4, :]
        scale = mod_ref[0, 4:5, :]
        xm_scr[...] = (x_ref[...] * (1.0 + scale) + shift).astype(BF16)

    h = jnp.dot(xm_scr[...], w_ref[...], preferred_element_type=F32)
    _cast_side_job(cast_in, cast_out)
    width = q_ref.shape[1]

    @pl.when(j == 0)
    def _():
        q_ref[...] = (h[:, :width] * Q_PRESCALE).astype(BF16)
        k_ref[...] = h[:, width:].astype(BF16)

    @pl.when(j == 1)
    def _():
        v_ref[...] = h[:, :width].astype(BF16)
        u_ref[...] = h[:, width:]


def _inproj(x2d, mod, w_in, *, seq, cast_jobs=(), tm=512):
    n, d = x2d.shape
    width = w_in.shape[1] // 4
    tiles_per_batch = seq // tm
    n_batch = n // seq
    grid = (n // tm, 2)
    out_spec = pl.BlockSpec((tm, width), lambda m, j: (m, 0))
    u_spec = pl.BlockSpec((tm, width), lambda m, j: (m % tiles_per_batch, m // tiles_per_batch))
    cast_specs = [_cast_job_spec(w, grid) for w in cast_jobs]
    out = pl.pallas_call(
        functools.partial(_inproj_kernel, n_cast=len(cast_jobs)),
        grid=grid,
        in_specs=[pl.BlockSpec((tm, d), lambda m, j: (m, 0)),
                  pl.BlockSpec((1, N_MOD, d), lambda m, j: (m // tiles_per_batch, 0, 0)),
                  pl.BlockSpec((d, 2 * width), lambda m, j: (0, j))] + cast_specs,
        out_specs=[out_spec, out_spec, out_spec, u_spec] + cast_specs,
        out_shape=([jax.ShapeDtypeStruct((n, width), BF16)] * 3 + [jax.ShapeDtypeStruct((seq, n_batch * width), F32)]
                   + [jax.ShapeDtypeStruct(w.shape, BF16) for w in cast_jobs]),
        scratch_shapes=[pltpu.VMEM((tm, d), BF16)],
        compiler_params=_params(("arbitrary", "arbitrary")),
        name="inproj",
    )(x2d, mod, w_in, *cast_jobs)
    return out[:4], out[4:]


def _dot_nt(a, b):
    return lax.dot_general(a, b, (((1,), (1,)), ((), ())), preferred_element_type=F32)


def _attn_kernel(*refs, n_blocks, hg, n_cast):
    slopes_ref, qa_ref, qb_ref, k_ref, v_ref = refs[:5]
    cast_in = refs[5:5 + n_cast]
    o_ref = refs[5 + n_cast]
    cast_out = refs[6 + n_cast:6 + 2 * n_cast]
    (kmean_scr, vt_scr, base_scr, q_scr, pk_scr, m_scr, acc_scr,
     sa_scr, sb_scr, pa_scr, pb_scr) = refs[6 + 2 * n_cast:]
    g = pl.program_id(1)
    p = pl.program_id(2)
    blk = MOBA_BLOCK
    dh = HEAD_DIM
    slope = [slopes_ref[g * hg + h] for h in range(hg)]
    hcols = [slice(h * dh, (h + 1) * dh) for h in range(hg)]
    q_refs = (qa_ref, qb_ref)
    tile_blk = (p, n_blocks - 1 - p)

    key_id = lax.broadcasted_iota(jnp.int32, (blk, blk), 0)
    qry_id = lax.broadcasted_iota(jnp.int32, (blk, blk), 1)
    rel = (qry_id - key_id).astype(F32)

    @pl.when(p == 0)
    def _():
        row_id = lax.broadcasted_iota(jnp.int32, (vt_scr.shape[1] - dh, vt_scr.shape[2]), 0)
        ones_row = jnp.where(row_id == 0, 1.0, 0.0).astype(BF16)
        for h in range(hg):
            base_scr[h] = (-LOG2E * slope[h]) * rel
            vt_scr[h, dh:, :] = ones_row
            for j in range(n_blocks):
                rows = slice(j * blk, (j + 1) * blk)
                kmean_scr[h, j:j + 1, :] = jnp.mean(k_ref[0, rows, hcols[h]].astype(F32), axis=0, keepdims=True)
                vt_scr[h, 0:dh, rows] = v_ref[0, rows, hcols[h]].astype(F32).T.astype(BF16)

    def item(n):
        tile = (n >= p).astype(jnp.int32)
        return tile, n - p * tile, jnp.where(tile == 0, tile_blk[0], tile_blk[1])

    def scores_into(n, s_scr):
        tile, j, _ = item(n)
        keys = pl.ds(pl.multiple_of(j * blk, blk), blk)
        for h in range(hg):
            s_scr[h] = _dot_nt(k_ref[0, keys, hcols[h]], q_scr[tile, h])

    def softmax_stage(n, s_scr, p_scr):
        tile, j, i_blk = item(n)
        j_f = j.astype(F32)
        i_f = i_blk.astype(F32)
        alphas = []
        for h in range(hg):
            selected = ((pk_scr[tile, h, 0:1, :] == j_f) | (pk_scr[tile, h, 1:2, :] == j_f)
                        | (pk_scr[tile, h, 2:3, :] == j_f))
            row_bias = jnp.where(selected, (j_f - i_f) * (float(blk) * LOG2E * slope[h]), NEG_INF)
            x = s_scr[h] + base_scr[h]
            m_old = m_scr[tile, h]
            m_new = jnp.maximum(m_old, jnp.max(x, axis=0, keepdims=True) + row_bias)
            p_scr[h] = jnp.exp2(x - (m_new - row_bias)).astype(BF16)
            m_scr[tile, h] = m_new
            alphas.append(jnp.exp2(m_old - m_new))
        return tuple(alphas)

    def accumulate(n, p_scr, alphas):
        tile, j, _ = item(n)
        keys = pl.ds(pl.multiple_of(j * blk, blk), blk)
        pv = [jnp.dot(vt_scr[h, :, keys], p_scr[h], preferred_element_type=F32) for h in range(hg)]
        for h in range(hg):
            acc_scr[tile, h] = alphas[h] * acc_scr[tile, h] + pv[h]

    for t in range(2):
        for h in range(hg):
            q_scr[t, h] = q_refs[t][0, 0, :, hcols[h]]
    scores_into(0, sa_scr)
    own = [pl.ds(pl.multiple_of(tile_blk[t] * blk, blk), blk) for t in range(2)]
    gates = [[_dot_nt(kmean_scr[h].astype(BF16), q_scr[t, h]) for h in range(hg)] for t in range(2)]
    s_own = [[_dot_nt(k_ref[0, own[t], hcols[h]], q_scr[t, h]) for h in range(hg)] for t in range(2)]
    _cast_side_job(cast_in, cast_out)
    p_own = [[None] * hg for _ in range(2)]
    for t in range(2):
        i_f = tile_blk[t].astype(F32)
        for h in range(hg):
            blk_id = lax.broadcasted_iota(jnp.int32, gates[t][h].shape, 0).astype(F32)
            work = jnp.where(blk_id < i_f, gates[t][h], NEG_INF)
            for r in range(MOBA_TOPK):
                mx = jnp.max(work, axis=0, keepdims=True)
                idx = jnp.min(jnp.where(work == mx, blk_id, float(n_blocks)), axis=0, keepdims=True)
                work = jnp.where(blk_id == idx, -jnp.inf, work)
                pk_scr[t, h, r:r + 1, :] = jnp.where(idx < i_f, idx, -1.0)
            s = jnp.where(rel >= 0.0, s_own[t][h] + base_scr[h], NEG_INF)
            m = jnp.max(s, axis=0, keepdims=True)
            m_scr[t, h] = m
            p_own[t][h] = jnp.exp2(s - m).astype(BF16)
    for t in range(2):
        pv_own = [jnp.dot(vt_scr[h, :, own[t]], p_own[t][h], preferred_element_type=F32) for h in range(hg)]
        for h in range(hg):
            acc_scr[t, h] = pv_own[h]

    s_slots = (sa_scr, sb_scr)
    p_slots = (pa_scr, pb_scr)
    a_prev = None
    for n in range(n_blocks - 1):
        if n + 1 < n_blocks - 1:
            scores_into(n + 1, s_slots[(n + 1) % 2])
        a_cur = softmax_stage(n, s_slots[n % 2], p_slots[n % 2])
        if n > 0:
            accumulate(n - 1, p_slots[(n - 1) % 2], a_prev)
        a_prev = a_cur
    accumulate(n_blocks - 2, p_slots[(n_blocks - 2) % 2], a_prev)
    for t in range(2):
        for h in range(hg):
            acc = acc_scr[t, h]
            o_ref[0, t, 0, :, hcols[h]] = (acc[0:dh] / acc[dh:dh + 1]).T


def _attention(q, k, v, slopes, cast_jobs=(), hg=4):
    b, s, width = q.shape
    n_heads = width // HEAD_DIM
    n_blocks = s // MOBA_BLOCK
    assert n_blocks % 2 == 0 and n_heads % hg == 0
    n_steps = n_blocks // 2
    gw = hg * HEAD_DIM
    ones_rows = 16
    kv_spec = pl.BlockSpec((1, s, gw), lambda bi, g, p: (bi, 0, g))
    slot = lambda dims, dt: pltpu.VMEM(dims, dt)
    grid = (b, n_heads // hg, n_steps)
    cast_specs = [_cast_job_spec(w, grid) for w in cast_jobs]
    out = pl.pallas_call(
        functools.partial(_attn_kernel, n_blocks=n_blocks, hg=hg, n_cast=len(cast_jobs)),
        grid=grid,
        in_specs=[pl.BlockSpec(memory_space=pltpu.SMEM),
                  pl.BlockSpec((1, 1, MOBA_BLOCK, gw), lambda bi, g, p: (bi, p, 0, g)),
                  pl.BlockSpec((1, 1, MOBA_BLOCK, gw), lambda bi, g, p: (bi, n_blocks - 1 - p, 0, g)),
                  kv_spec, kv_spec] + cast_specs,
        out_specs=[pl.BlockSpec((1, 2, 1, MOBA_BLOCK, gw), lambda bi, g, p: (bi, 0, p, 0, g))] + cast_specs,
        out_shape=([jax.ShapeDtypeStruct((b, 2, n_steps, MOBA_BLOCK, width), F32)]
                   + [jax.ShapeDtypeStruct(w.shape, BF16) for w in cast_jobs]),
        scratch_shapes=[slot((hg, n_blocks, HEAD_DIM), F32),
                        slot((hg, HEAD_DIM + ones_rows, s), BF16),
                        slot((hg, MOBA_BLOCK, MOBA_BLOCK), F32),
                        slot((2, hg, MOBA_BLOCK, HEAD_DIM), BF16),
                        slot((2, hg, SUBLANES, MOBA_BLOCK), F32),
                        slot((2, hg, 1, MOBA_BLOCK), F32),
                        slot((2, hg, HEAD_DIM + ones_rows, MOBA_BLOCK), F32),
                        slot((hg, MOBA_BLOCK, MOBA_BLOCK), F32), slot((hg, MOBA_BLOCK, MOBA_BLOCK), F32),
                        slot((hg, MOBA_BLOCK, MOBA_BLOCK), BF16), slot((hg, MOBA_BLOCK, MOBA_BLOCK), BF16)],
        compiler_params=_params(("arbitrary", "arbitrary", "arbitrary")),
        name="moba_attn",
    )(slopes, q.reshape(b, n_blocks, MOBA_BLOCK, width), q.reshape(b, n_blocks, MOBA_BLOCK, width), k, v, *cast_jobs)
    return out[0], out[1:]


def _ssm_tables(lam_re, lam_im, log_dt, b_re, b_im, c_re, c_im):
    g_total, p = lam_re.shape
    n_slab = g_total * SSM_GROUP // SLAB
    gps = SLAB // SSM_GROUP
    gph = gps // 2
    dt = jnp.exp(log_dt)[:, None]
    decay = jnp.exp(lam_re * dt)
    ab_re = decay * jnp.cos(lam_im * dt)
    ab_im = decay * jnp.sin(lam_im * dt)
    den = lam_re * lam_re + lam_im * lam_im
    coef_re = ((ab_re - 1.0) * lam_re + ab_im * lam_im) / den
    coef_im = (ab_im * lam_re - (ab_re - 1.0) * lam_im) / den
    bb_re = coef_re[..., None] * b_re - coef_im[..., None] * b_im
    bb_im = coef_re[..., None] * b_im + coef_im[..., None] * b_re
    eye = jnp.eye(gps, dtype=F32)

    bbri = jnp.stack([bb_re, bb_im], axis=0).reshape(2, n_slab, gps, p, SSM_GROUP)
    bb = jnp.einsum('rsgph,gk->sghkrp', bbri, eye)
    bb = bb.reshape(n_slab, gps, SSM_GROUP, 2, gph, 2, p).transpose(0, 1, 2, 3, 5, 4, 6)
    bb = bb.reshape(n_slab, SLAB, 2, 2 * HALF_STATES).transpose(0, 2, 1, 3).reshape(n_slab, 2 * SLAB, 2 * HALF_STATES)

    ccri = jnp.stack([c_re, -c_im], axis=0).reshape(2, n_slab, gps, SSM_GROUP, p)
    cc = jnp.einsum('rsghp,gk->skrpgh', ccri, eye)
    cc = cc.reshape(n_slab, 2, gph, 2, p, gps, SSM_GROUP).transpose(0, 1, 3, 2, 4, 5, 6)
    cc = cc.reshape(n_slab, 2, 2 * HALF_STATES, SLAB).transpose(0, 2, 1, 3).reshape(n_slab, 2 * HALF_STATES, 2 * SLAB)

    def rows(a):
        a = a.reshape(n_slab, 2, 1, HALF_STATES)
        return jnp.broadcast_to(a, (n_slab, 2, 4, HALF_STATES)).reshape(n_slab, SUBLANES, HALF_STATES)

    return bb.astype(BF16), cc.astype(BF16), rows(ab_re), rows(ab_im)


def _ssm_kernel(u_ref, bb_ref, cc_ref, are_ref, aim_ref, d_ref, y_ref, x_scr, carry_scr, *, t_chunk):
    c = pl.program_id(0)
    s = pl.program_id(1)
    hs = HALF_STATES
    n_cg = 2 * hs // LANES
    n_re = hs // LANES
    half_rows = SUBLANES // 2


    @pl.when(c == 0)
    def _():
        carry_scr[s] = jnp.zeros((n_cg, SUBLANES, LANES), F32)

    a_re = [are_ref[0, :, g * LANES:(g + 1) * LANES] for g in range(n_re)]
    a_im = [aim_ref[0, :, g * LANES:(g + 1) * LANES] for g in range(n_re)]
    sub = t_chunk // SCAN_SUBCHUNKS

    def project_in(k):
        part = slice(k * sub * SUBLANES, (k + 1) * sub * SUBLANES)
        u4 = u_ref[k * sub:(k + 1) * sub]
        u3 = jnp.concatenate([u4, jnp.zeros_like(u4)], axis=1)
        u_lo = u3.reshape(sub * SUBLANES, SLAB)
        u_hi = pltpu.roll(u3, half_rows, 1).reshape(sub * SUBLANES, SLAB)
        lhs = jnp.concatenate([u_lo, u_hi], axis=1).astype(BF16)
        bu = jnp.dot(lhs, bb_ref[0], preferred_element_type=F32)
        for g in range(n_cg):
            x_scr[g, part, :] = bu[:, g * LANES:(g + 1) * LANES]

    def scan(k, x):
        for t in range(k * sub, (k + 1) * sub):
            idx = slice(t * SUBLANES, (t + 1) * SUBLANES)
            new = [None] * n_cg
            for g in range(n_re):
                x_re, x_im = x[g], x[n_re + g]
                new[g] = a_re[g] * x_re - a_im[g] * x_im + x_scr[g, idx, :]
                new[n_re + g] = a_re[g] * x_im + a_im[g] * x_re + x_scr[n_re + g, idx, :]
            for g in range(n_cg):
                x_scr[g, idx, :] = new[g]
            x = new
        return x

    def project_out(k):
        part = slice(k * sub * SUBLANES, (k + 1) * sub * SUBLANES)
        steps = slice(k * sub, (k + 1) * sub)
        xs = jnp.concatenate([x_scr[g, part, :] for g in range(n_cg)], axis=1).astype(BF16)
        yy = jnp.dot(xs, cc_ref[0], preferred_element_type=F32)
        y_half1 = pltpu.roll(yy[:, SLAB:].reshape(sub, SUBLANES, SLAB), half_rows, 1)
        y3 = yy[:, :SLAB].reshape(sub, SUBLANES, SLAB) + y_half1
        y_ref[steps] = y3[:, 0:half_rows, :] + d_ref[...] * u_ref[steps]

    x = [carry_scr[s, g] for g in range(n_cg)]
    project_in(0)
    for k in range(SCAN_SUBCHUNKS):
        if k + 1 < SCAN_SUBCHUNKS:
            project_in(k + 1)
        x = scan(k, x)
        if k > 0:
            project_out(k - 1)
    project_out(SCAN_SUBCHUNKS - 1)
    for g in range(n_cg):
        carry_scr[s, g] = x[g]


def _ssm(u, bb, cc, a_re, a_im, d_skip, t_chunk=256):
    s, r, width = u.shape
    assert r == SUBLANES // 2
    n_slab = width // SLAB
    hs = HALF_STATES
    return pl.pallas_call(
        functools.partial(_ssm_kernel, t_chunk=t_chunk),
        grid=(s // t_chunk, n_slab),
        in_specs=[pl.BlockSpec((t_chunk, SUBLANES // 2, SLAB), lambda c, sl: (c, 0, sl)),
                  pl.BlockSpec((1, 2 * SLAB, 2 * hs), lambda c, sl: (sl, 0, 0)),
                  pl.BlockSpec((1, 2 * hs, 2 * SLAB), lambda c, sl: (sl, 0, 0)),
                  pl.BlockSpec((1, SUBLANES, hs), lambda c, sl: (sl, 0, 0)),
                  pl.BlockSpec((1, SUBLANES, hs), lambda c, sl: (sl, 0, 0)),
                  pl.BlockSpec((1, SLAB), lambda c, sl: (0, sl))],
        out_specs=pl.BlockSpec((t_chunk, SUBLANES // 2, SLAB), lambda c, sl: (c, 0, sl)),
        out_shape=jax.ShapeDtypeStruct((s, SUBLANES // 2, width), F32),
        scratch_shapes=[pltpu.VMEM((2 * hs // LANES, SUBLANES * t_chunk, LANES), F32),
                        pltpu.VMEM((n_slab, 2 * hs // LANES, SUBLANES, LANES), F32)],
        compiler_params=_params(("arbitrary", "arbitrary")),
        name="s5_scan",
    )(u, bb, cc, a_re, a_im, d_skip.reshape(1, width))


def _mix_kernel(a0_ref, a1_ref, y_ref, x_ref, mod_ref, wglu_ref, bglu_ref, ga_ref, gs_ref, wo_ref, lng_ref, lnb_ref,
                o_ref):
    aw = a0_ref.shape[1]
    a = _rms_norm_rows(jnp.concatenate([a0_ref[...], a1_ref[...]], axis=0), ga_ref[...])
    y = jax.nn.gelu(y_ref[...])
    z = y * jax.nn.sigmoid(jnp.dot(y.astype(BF16), wglu_ref[...], preferred_element_type=F32) + bglu_ref[...])
    sn = _rms_norm_rows(z, gs_ref[...])
    mix = (jnp.dot(a.astype(BF16), wo_ref[0:aw, :], preferred_element_type=F32)
           + jnp.dot(sn.astype(BF16), wo_ref[aw:, :], preferred_element_type=F32))
    gate = mod_ref[0, 5:6, :]
    o_ref[...] = _residual_layer_norm(x_ref[...], mix, 1.0 + gate, lng_ref[...], lnb_ref[...])


def _mix(attn, y_rows, x2d, mod, w_glu, b_glu, g_attn, g_ssm, w_out, ln_g, ln_b, *, seq):
    n, d = x2d.shape
    _, _, n_steps, blk, aw = attn.shape
    tm = 2 * blk
    sw = y_rows.shape[1] // (n // seq)
    tiles_per_batch = seq // tm
    const = lambda m: (0, 0)

    def attn_spec(e):
        def index(m):
            i = 2 * (m % tiles_per_batch) + e
            folded = i >= n_steps
            return (m // tiles_per_batch, folded.astype(jnp.int32), jnp.where(folded, 2 * n_steps - 1 - i, i), 0, 0)
        return pl.BlockSpec((None, None, None, blk, aw), index)

    return pl.pallas_call(
        _mix_kernel,
        grid=(n // tm,),
        in_specs=[attn_spec(0), attn_spec(1),
                  pl.BlockSpec((tm, sw), lambda m: (m % tiles_per_batch, m // tiles_per_batch)),
                  pl.BlockSpec((tm, d), lambda m: (m, 0)),
                  pl.BlockSpec((1, N_MOD, d), lambda m: (m // tiles_per_batch, 0, 0)),
                  pl.BlockSpec((sw, sw), const),
                  pl.BlockSpec((1, sw), const),
                  pl.BlockSpec((1, aw), const),
                  pl.BlockSpec((1, sw), const),
                  pl.BlockSpec((aw + sw, d), const),
                  pl.BlockSpec((1, d), const),
                  pl.BlockSpec((1, d), const)],
        out_specs=pl.BlockSpec((tm, d), lambda m: (m, 0)),
        out_shape=jax.ShapeDtypeStruct((n, d), F32),
        compiler_params=_params(("parallel",)),
        name="mix",
    )(attn, attn, y_rows, x2d, mod, w_glu, b_glu.reshape(1, sw), g_attn.reshape(1, aw), g_ssm.reshape(1, sw),
      w_out, ln_g.reshape(1, d), ln_b.reshape(1, d))


def kernel(x, c, w_ada, b_ada, ffn1_w_gate, ffn1_w_up, ffn1_w_down, ln1_g, ln1_b, w_in, attn_norm_g, ssm_lambda_re, ssm_lambda_im, ssm_log_dt, ssm_b_re, ssm_b_im, ssm_c_re, ssm_c_im, ssm_d, ssm_w_glu, ssm_b_glu, ssm_norm_g, w_out, ln2_g, ln2_b, ffn2_w_gate, ffn2_w_up, ffn2_w_down, ln3_g, ln3_b):
    b, s, d = x.shape
    assert w_ada.shape[0] == DEPTH
    slopes = 2.0 ** (-(8.0 / N_HEADS) * jnp.arange(1, N_HEADS + 1, dtype=F32))
    c_pad = jnp.pad(c, ((0, SUBLANES - b), (0, 0)))
    x2d = x.reshape(b * s, d)
    for l in range(DEPTH):
        mod = _adaln(c_pad, w_ada[l], b_ada[l])[:b].reshape(b, N_MOD, d)

        x2d, (w_in_bf, w_out_bf, w_glu_bf) = _ffn(
            x2d, mod, ffn1_w_gate[l].astype(BF16), ffn1_w_up[l].astype(BF16), ffn1_w_down[l].astype(BF16),
            ln1_g[l], ln1_b[l], mod_base=0, seq=s, cast_jobs=(w_in[l], w_out[l], ssm_w_glu[l]))

        (q, k, v, u), _ = _inproj(x2d, mod, w_in_bf, seq=s)
        aw = q.shape[1]
        attn, (w2_gate, w2_up, w2_down) = _attention(
            q.reshape(b, s, aw), k.reshape(b, s, aw), v.reshape(b, s, aw), slopes,
            cast_jobs=(ffn2_w_gate[l], ffn2_w_up[l], ffn2_w_down[l]))

        bb, cc, a_re, a_im = _ssm_tables(ssm_lambda_re[l], ssm_lambda_im[l], ssm_log_dt[l],
                                         ssm_b_re[l], ssm_b_im[l], ssm_c_re[l], ssm_c_im[l])
        sw = u.shape[1] // b
        y = _ssm(u.reshape(s, b, sw), bb, cc, a_re, a_im, ssm_d[l])

        x2d = _mix(attn, y.reshape(s, b * sw), x2d, mod, w_glu_bf, ssm_b_glu[l],
                   attn_norm_g[l], ssm_norm_g[l], w_out_bf, ln2_g[l], ln2_b[l], seq=s)

        x2d, _ = _ffn(x2d, mod, w2_gate, w2_up, w2_down, ln3_g[l], ln3_b[l], mod_base=6, seq=s)
    return x2d.reshape(b, s, d)
```

```python
import functools
import math

import jax
import jax.numpy as jnp
from jax import lax
from jax.experimental import pallas as pl
from jax.experimental.pallas import tpu as pltpu

F32 = jnp.float32
BF16 = jnp.bfloat16

N_HEADS = 8
HEAD_DIM = 128
MOBA_BLOCK = 256
MOBA_TOPK = 3
SSM_GROUP = 16
SSM_STATE = 64
N_MOD = 9
LN_EPS = 1e-5
NEG_INF = -1e30
LOG2E = math.log2(math.e)
DEPTH = 1
ALPHA = (2.0 * DEPTH) ** 0.25
Q_PRESCALE = HEAD_DIM ** -0.5 * LOG2E

LANES = 128
SUBLANES = 8
VMEM_LIMIT_BYTES = 56 * 1024 * 1024

SLAB = LANES
HALF_STATES = 256
SCAN_SUBCHUNKS = 4
FFN_TILE = 512


def _params(sem):
    return pltpu.CompilerParams(dimension_semantics=sem, vmem_limit_bytes=VMEM_LIMIT_BYTES)


def _residual_layer_norm(x, y, coef, g, b):
    r = x + (coef * (1.0 / ALPHA)) * y
    mu = jnp.mean(r, axis=-1, keepdims=True)
    d = r - mu
    var = jnp.mean(d * d, axis=-1, keepdims=True)
    return d * lax.rsqrt(var + LN_EPS / (ALPHA * ALPHA)) * g + b


def _rms_norm_rows(y, g):
    return y * lax.rsqrt(jnp.mean(y * y, axis=-1, keepdims=True) + LN_EPS) * g


def _adaln_kernel(c_ref, w_ref, b_ref, o_ref):
    c = c_ref[...]
    c_act = (c * jax.nn.sigmoid(c)).astype(BF16)
    o_ref[...] = jnp.dot(c_act, w_ref[...].astype(BF16), preferred_element_type=F32) + b_ref[...]


def _adaln(c_pad, w_ada, b_ada, tn=1024):
    rows, d = c_pad.shape
    n = w_ada.shape[1]
    return pl.pallas_call(
        _adaln_kernel,
        grid=(n // tn,),
        in_specs=[pl.BlockSpec((rows, d), lambda j: (0, 0)),
                  pl.BlockSpec((d, tn), lambda j: (0, j)),
                  pl.BlockSpec((1, tn), lambda j: (0, j))],
        out_specs=pl.BlockSpec((rows, tn), lambda j: (0, j)),
        out_shape=jax.ShapeDtypeStruct((rows, n), F32),
        compiler_params=_params(("arbitrary",)),
        name="adaln",
    )(c_pad, w_ada, b_ada.reshape(1, n))


def _cast_side_job(cast_in, cast_out):
    for src, dst in zip(cast_in, cast_out):
        if len(dst.shape) == 2:
            dst[...] = src[...].astype(BF16)
        else:
            width = dst.shape[2]
            for j in range(dst.shape[0]):
                dst[j] = src[:, j * width:(j + 1) * width].astype(BF16)


def _ffn_kernel(*refs, mod_base, n_cast, odd_tiles):
    x_ref, mod_ref, wga_ref, wua_ref, wda_ref, wgb_ref, wub_ref, wdb_ref, lng_ref, lnb_ref = refs[:10]
    cast_in = refs[10:10 + n_cast]
    o_ref = refs[10 + n_cast]
    cast_out = refs[11 + n_cast:11 + 2 * n_cast]
    u_scr, acc_scr = refs[11 + 2 * n_cast:]
    f = pl.program_id(1)
    last = pl.num_programs(1) - 1

    @pl.when(f == 0)
    def _():
        shift = mod_ref[0, mod_base:mod_base + 1, :]
        scale = mod_ref[0, mod_base + 1:mod_base + 2, :]
        u_scr[...] = (x_ref[...] * (1.0 + scale) + shift).astype(BF16)
        acc_scr[...] = jnp.zeros_like(acc_scr)

    def swiglu(wg_ref, wu_ref, wd_ref):
        u = u_scr[...]
        g = jnp.dot(u, wg_ref[...], preferred_element_type=F32)
        up = jnp.dot(u, wu_ref[...], preferred_element_type=F32)
        h = (g * jax.nn.sigmoid(g)) * up
        return jnp.dot(h.astype(BF16), wd_ref[...], preferred_element_type=F32)

    def both_tiles():
        y = swiglu(wga_ref, wua_ref, wda_ref)
        _cast_side_job(cast_in, cast_out)
        return y + swiglu(wgb_ref, wub_ref, wdb_ref)

    def finish(y):
        gate = mod_ref[0, mod_base + 2:mod_base + 3, :]
        o_ref[...] = _residual_layer_norm(x_ref[...], acc_scr[...] + y, 0.5 * (1.0 + gate), lng_ref[...], lnb_ref[...])

    @pl.when(f < last)
    def _():
        acc_scr[...] += both_tiles()

    @pl.when(f == last)
    def _():
        if odd_tiles:
            _cast_side_job(cast_in, cast_out)
            finish(swiglu(wga_ref, wua_ref, wda_ref))
        else:
            finish(both_tiles())


def _col_tiles(w, width):
    rows, cols = w.shape
    return w.reshape(rows, cols // width, width).transpose(1, 0, 2)


def _cast_job_specs(job, grid):
    w, col_width = job if isinstance(job, tuple) else (job, None)
    rows, cols = w.shape
    bf16_rows = 2 * SUBLANES
    steps = math.prod(grid)
    n_parts = max(k for k in range(1, steps + 1)
                  if steps % k == 0 and rows % k == 0 and (rows // k) % bf16_rows == 0)
    visits = steps // n_parts

    def row_block(*ids):
        step = 0
        for i, extent in zip(ids, grid):
            step = step * extent + i
        return step // visits

    in_spec = pl.BlockSpec((rows // n_parts, cols), lambda *ids: (row_block(*ids), 0))
    if col_width is None:
        return in_spec, in_spec, jax.ShapeDtypeStruct(w.shape, BF16)
    n_tiles = cols // col_width
    out_spec = pl.BlockSpec((n_tiles, rows // n_parts, col_width), lambda *ids: (0, row_block(*ids), 0))
    return in_spec, out_spec, jax.ShapeDtypeStruct((n_tiles, rows, col_width), BF16)


def _ffn(x2d, mod, wg, wu, wd, ln_g, ln_b, *, mod_base, seq, cast_jobs=(), tm=512):
    n, d = x2d.shape
    n_tiles, _, tf = wg.shape
    tiles_per_batch = seq // tm
    grid = (n // tm, (n_tiles + 1) // 2)
    tile_a = lambda f: 2 * f
    tile_b = lambda f: jnp.minimum(2 * f + 1, n_tiles - 1)
    cast_in, cast_out, cast_shapes = zip(*[_cast_job_specs(job, grid) for job in cast_jobs]) if cast_jobs else ((), (), ())
    out = pl.pallas_call(
        functools.partial(_ffn_kernel, mod_base=mod_base, n_cast=len(cast_jobs), odd_tiles=n_tiles % 2 == 1),
        grid=grid,
        in_specs=[pl.BlockSpec((tm, d), lambda m, f: (m, 0)),
                  pl.BlockSpec((1, N_MOD, d), lambda m, f: (m // tiles_per_batch, 0, 0)),
                  pl.BlockSpec((None, d, tf), lambda m, f: (tile_a(f), 0, 0)),
                  pl.BlockSpec((None, d, tf), lambda m, f: (tile_a(f), 0, 0)),
                  pl.BlockSpec((tf, d), lambda m, f: (tile_a(f), 0)),
                  pl.BlockSpec((None, d, tf), lambda m, f: (tile_b(f), 0, 0)),
                  pl.BlockSpec((None, d, tf), lambda m, f: (tile_b(f), 0, 0)),
                  pl.BlockSpec((tf, d), lambda m, f: (tile_b(f), 0)),
                  pl.BlockSpec((1, d), lambda m, f: (0, 0)),
                  pl.BlockSpec((1, d), lambda m, f: (0, 0))] + list(cast_in),
        out_specs=[pl.BlockSpec((tm, d), lambda m, f: (m, 0))] + list(cast_out),
        out_shape=[jax.ShapeDtypeStruct((n, d), F32)] + list(cast_shapes),
        scratch_shapes=[pltpu.VMEM((tm, d), BF16), pltpu.VMEM((tm, d), F32)],
        compiler_params=_params(("arbitrary", "arbitrary")),
        name="ffn",
    )(x2d, mod, wg, wu, wd, wg, wu, wd, ln_g.reshape(1, d), ln_b.reshape(1, d),
      *[job[0] if isinstance(job, tuple) else job for job in cast_jobs])
    return out[0], out[1:]


def _inproj_kernel(x_ref, mod_ref, w_ref, q_ref, k_ref, v_ref, u_ref):
    width = q_ref.shape[1]
    shift = mod_ref[0, 3:4, :]
    scale = mod_ref[0, 4:5, :]
    xm = (x_ref[...] * (1.0 + scale) + shift).astype(BF16)

    def project(col):
        return jnp.dot(xm, w_ref[:, col * width:(col + 1) * width], preferred_element_type=F32)

    q_ref[...] = (project(0) * Q_PRESCALE).astype(BF16)
    k_ref[...] = project(1).astype(BF16)
    v_ref[...] = project(2).astype(BF16)
    u_ref[...] = project(3)


def _inproj(x2d, mod, w_in, *, seq, tm=512):
    n, d = x2d.shape
    width = w_in.shape[1] // 4
    tiles_per_batch = seq // tm
    n_batch = n // seq
    out_spec = pl.BlockSpec((tm, width), lambda m: (m, 0))
    u_spec = pl.BlockSpec((tm, width), lambda m: (m % tiles_per_batch, m // tiles_per_batch))
    return pl.pallas_call(
        _inproj_kernel,
        grid=(n // tm,),
        in_specs=[pl.BlockSpec((tm, d), lambda m: (m, 0)),
                  pl.BlockSpec((1, N_MOD, d), lambda m: (m // tiles_per_batch, 0, 0)),
                  pl.BlockSpec((d, 4 * width), lambda m: (0, 0), pipeline_mode=pl.Buffered(1))],
        out_specs=[out_spec, out_spec, out_spec, u_spec],
        out_shape=[jax.ShapeDtypeStruct((n, width), BF16)] * 3 + [jax.ShapeDtypeStruct((seq, n_batch * width), F32)],
        compiler_params=_params(("arbitrary",)),
        name="inproj",
    )(x2d, mod, w_in)


def _dot_nt(a, b):
    return lax.dot_general(a, b, (((1,), (1,)), ((), ())), preferred_element_type=F32)


def _attn_kernel(*refs, n_blocks, hg, n_cast):
    slopes_ref, qa_ref, qb_ref, k_ref, v_ref = refs[:5]
    cast_in = refs[5:5 + n_cast]
    o_ref = refs[5 + n_cast]
    cast_out = refs[6 + n_cast:6 + 2 * n_cast]
    (kmean_scr, vt_scr, base_scr, q_scr, pk_scr, m_scr, acc_scr,
     sa_scr, sb_scr, pa_scr, pb_scr) = refs[6 + 2 * n_cast:]
    g = pl.program_id(1)
    p = pl.program_id(2)
    blk = MOBA_BLOCK
    dh = HEAD_DIM
    slope = [slopes_ref[g * hg + h] for h in range(hg)]
    hcols = [slice(h * dh, (h + 1) * dh) for h in range(hg)]
    q_refs = (qa_ref, qb_ref)
    tile_blk = (p, n_blocks - 1 - p)

    key_id = lax.broadcasted_iota(jnp.int32, (blk, blk), 0)
    qry_id = lax.broadcasted_iota(jnp.int32, (blk, blk), 1)
    rel = (qry_id - key_id).astype(F32)

    @pl.when(p == 0)
    def _():
        row_id = lax.broadcasted_iota(jnp.int32, (vt_scr.shape[1] - dh, vt_scr.shape[2]), 0)
        ones_row = jnp.where(row_id == 0, 1.0, 0.0).astype(BF16)
        for h in range(hg):
            base_scr[h] = (-LOG2E * slope[h]) * rel
            vt_scr[h, dh:, :] = ones_row
            for j in range(n_blocks):
                rows = slice(j * blk, (j + 1) * blk)
                kmean_scr[h, j:j + 1, :] = jnp.mean(k_ref[0, rows, hcols[h]].astype(F32), axis=0, keepdims=True)
                vt_scr[h, 0:dh, rows] = v_ref[0, rows, hcols[h]].astype(F32).T.astype(BF16)

    def item(n):
        tile = (n >= p).astype(jnp.int32)
        return tile, n - p * tile, jnp.where(tile == 0, tile_blk[0], tile_blk[1])

    def scores_into(n, s_scr):
        tile, j, _ = item(n)
        keys = pl.ds(pl.multiple_of(j * blk, blk), blk)
        for h in range(hg):
            s_scr[h] = _dot_nt(k_ref[0, keys, hcols[h]], q_scr[tile, h])

    def softmax_stage(n, s_scr, p_scr):
        tile, j, i_blk = item(n)
        j_f = j.astype(F32)
        i_f = i_blk.astype(F32)
        alphas = []
        for h in range(hg):
            selected = ((pk_scr[tile, h, 0:1, :] == j_f) | (pk_scr[tile, h, 1:2, :] == j_f)
                        | (pk_scr[tile, h, 2:3, :] == j_f))
            row_bias = jnp.where(selected, (j_f - i_f) * (float(blk) * LOG2E * slope[h]), NEG_INF)
            x = s_scr[h] + base_scr[h]
            m_old = m_scr[tile, h]
            m_new = jnp.maximum(m_old, jnp.max(x, axis=0, keepdims=True) + row_bias)
            p_scr[h] = jnp.exp2(x - (m_new - row_bias)).astype(BF16)
            m_scr[tile, h] = m_new
            alphas.append(jnp.exp2(m_old - m_new))
        return tuple(alphas)

    def accumulate(n, p_scr, alphas):
        tile, j, _ = item(n)
        keys = pl.ds(pl.multiple_of(j * blk, blk), blk)
        pv = [jnp.dot(vt_scr[h, :, keys], p_scr[h], preferred_element_type=F32) for h in range(hg)]
        for h in range(hg):
            acc_scr[tile, h] = alphas[h] * acc_scr[tile, h] + pv[h]

    for t in range(2):
        for h in range(hg):
            q_scr[t, h] = q_refs[t][0, 0, :, hcols[h]]
    scores_into(0, sa_scr)
    own = [pl.ds(pl.multiple_of(tile_blk[t] * blk, blk), blk) for t in range(2)]
    gates = [[_dot_nt(kmean_scr[h].astype(BF16), q_scr[t, h]) for h in range(hg)] for t in range(2)]
    s_own = [[_dot_nt(k_ref[0, own[t], hcols[h]], q_scr[t, h]) for h in range(hg)] for t in range(2)]
    _cast_side_job(cast_in, cast_out)
    p_own = [[None] * hg for _ in range(2)]
    for t in range(2):
        i_f = tile_blk[t].astype(F32)
        for h in range(hg):
            blk_id = lax.broadcasted_iota(jnp.int32, gates[t][h].shape, 0).astype(F32)
            work = jnp.where(blk_id < i_f, gates[t][h], NEG_INF)
            for r in range(MOBA_TOPK):
                mx = jnp.max(work, axis=0, keepdims=True)
                idx = jnp.min(jnp.where(work == mx, blk_id, float(n_blocks)), axis=0, keepdims=True)
                work = jnp.where(blk_id == idx, -jnp.inf, work)
                pk_scr[t, h, r:r + 1, :] = jnp.where(idx < i_f, idx, -1.0)
            s = jnp.where(rel >= 0.0, s_own[t][h] + base_scr[h], NEG_INF)
            m = jnp.max(s, axis=0, keepdims=True)
            m_scr[t, h] = m
            p_own[t][h] = jnp.exp2(s - m).astype(BF16)
    for t in range(2):
        pv_own = [jnp.dot(vt_scr[h, :, own[t]], p_own[t][h], preferred_element_type=F32) for h in range(hg)]
        for h in range(hg):
            acc_scr[t, h] = pv_own[h]

    s_slots = (sa_scr, sb_scr)
    p_slots = (pa_scr, pb_scr)
    a_prev = None
    for n in range(n_blocks - 1):
        if n + 1 < n_blocks - 1:
            scores_into(n + 1, s_slots[(n + 1) % 2])
        a_cur = softmax_stage(n, s_slots[n % 2], p_slots[n % 2])
        if n > 0:
            accumulate(n - 1, p_slots[(n - 1) % 2], a_prev)
        a_prev = a_cur
    accumulate(n_blocks - 2, p_slots[(n_blocks - 2) % 2], a_prev)
    for t in range(2):
        for h in range(hg):
            acc = acc_scr[t, h]
            o_ref[0, t, 0, :, hcols[h]] = (acc[0:dh] / acc[dh:dh + 1]).T


def _attention(q, k, v, slopes, cast_jobs=(), hg=4):
    b, s, width = q.shape
    n_heads = width // HEAD_DIM
    n_blocks = s // MOBA_BLOCK
    assert n_blocks % 2 == 0 and n_heads % hg == 0
    n_steps = n_blocks // 2
    gw = hg * HEAD_DIM
    ones_rows = 16
    kv_spec = pl.BlockSpec((1, s, gw), lambda bi, g, p: (bi, 0, g))
    slot = lambda dims, dt: pltpu.VMEM(dims, dt)
    grid = (b, n_heads // hg, n_steps)
    cast_in, cast_out, cast_shapes = zip(*[_cast_job_specs(job, grid) for job in cast_jobs]) if cast_jobs else ((), (), ())
    out = pl.pallas_call(
        functools.partial(_attn_kernel, n_blocks=n_blocks, hg=hg, n_cast=len(cast_jobs)),
        grid=grid,
        in_specs=[pl.BlockSpec(memory_space=pltpu.SMEM),
                  pl.BlockSpec((1, 1, MOBA_BLOCK, gw), lambda bi, g, p: (bi, p, 0, g)),
                  pl.BlockSpec((1, 1, MOBA_BLOCK, gw), lambda bi, g, p: (bi, n_blocks - 1 - p, 0, g)),
                  kv_spec, kv_spec] + list(cast_in),
        out_specs=[pl.BlockSpec((1, 2, 1, MOBA_BLOCK, gw), lambda bi, g, p: (bi, 0, p, 0, g))] + list(cast_out),
        out_shape=[jax.ShapeDtypeStruct((b, 2, n_steps, MOBA_BLOCK, width), F32)] + list(cast_shapes),
        scratch_shapes=[slot((hg, n_blocks, HEAD_DIM), F32),
                        slot((hg, HEAD_DIM + ones_rows, s), BF16),
                        slot((hg, MOBA_BLOCK, MOBA_BLOCK), F32),
                        slot((2, hg, MOBA_BLOCK, HEAD_DIM), BF16),
                        slot((2, hg, SUBLANES, MOBA_BLOCK), F32),
                        slot((2, hg, 1, MOBA_BLOCK), F32),
                        slot((2, hg, HEAD_DIM + ones_rows, MOBA_BLOCK), F32),
                        slot((hg, MOBA_BLOCK, MOBA_BLOCK), F32), slot((hg, MOBA_BLOCK, MOBA_BLOCK), F32),
                        slot((hg, MOBA_BLOCK, MOBA_BLOCK), BF16), slot((hg, MOBA_BLOCK, MOBA_BLOCK), BF16)],
        compiler_params=_params(("arbitrary", "arbitrary", "arbitrary")),
        name="moba_attn",
    )(slopes, q.reshape(b, n_blocks, MOBA_BLOCK, width), q.reshape(b, n_blocks, MOBA_BLOCK, width), k, v,
      *[job[0] if isinstance(job, tuple) else job for job in cast_jobs])
    return out[0], out[1:]


def _ssm_tables(lam_re, lam_im, log_dt, b_re, b_im, c_re, c_im):
    g_total, p = lam_re.shape
    n_slab = g_total * SSM_GROUP // SLAB
    gps = SLAB // SSM_GROUP
    gph = gps // 2
    dt = jnp.exp(log_dt)[:, None]
    decay = jnp.exp(lam_re * dt)
    ab_re = decay * jnp.cos(lam_im * dt)
    ab_im = decay * jnp.sin(lam_im * dt)
    den = lam_re * lam_re + lam_im * lam_im
    coef_re = ((ab_re - 1.0) * lam_re + ab_im * lam_im) / den
    coef_im = (ab_im * lam_re - (ab_re - 1.0) * lam_im) / den
    bb_re = coef_re[..., None] * b_re - coef_im[..., None] * b_im
    bb_im = coef_re[..., None] * b_im + coef_im[..., None] * b_re
    eye = jnp.eye(gps, dtype=F32)

    bbri = jnp.stack([bb_re, bb_im], axis=0).reshape(2, n_slab, gps, p, SSM_GROUP)
    bb = jnp.einsum('rsgph,gk->sghkrp', bbri, eye)
    bb = bb.reshape(n_slab, gps, SSM_GROUP, 2, gph, 2, p).transpose(0, 1, 2, 3, 5, 4, 6)
    bb = bb.reshape(n_slab, SLAB, 2, 2 * HALF_STATES).transpose(0, 2, 1, 3).reshape(n_slab, 2 * SLAB, 2 * HALF_STATES)

    ccri = jnp.stack([c_re, -c_im], axis=0).reshape(2, n_slab, gps, SSM_GROUP, p)
    cc = jnp.einsum('rsghp,gk->skrpgh', ccri, eye)
    cc = cc.reshape(n_slab, 2, gph, 2, p, gps, SSM_GROUP).transpose(0, 1, 3, 2, 4, 5, 6)
    cc = cc.reshape(n_slab, 2, 2 * HALF_STATES, SLAB).transpose(0, 2, 1, 3).reshape(n_slab, 2 * HALF_STATES, 2 * SLAB)

    def rows(a):
        a = a.reshape(n_slab, 2, 1, HALF_STATES)
        return jnp.broadcast_to(a, (n_slab, 2, 4, HALF_STATES)).reshape(n_slab, SUBLANES, HALF_STATES)

    return bb.astype(BF16), cc.astype(BF16), rows(ab_re), rows(ab_im)


def _ssm_kernel(u_ref, bb_ref, cc_ref, are_ref, aim_ref, d_ref, y_ref, x_scr, carry_scr, *, t_chunk):
    c = pl.program_id(0)
    s = pl.program_id(1)
    hs = HALF_STATES
    n_cg = 2 * hs // LANES
    n_re = hs // LANES
    half_rows = SUBLANES // 2


    @pl.when(c == 0)
    def _():
        carry_scr[s] = jnp.zeros((n_cg, SUBLANES, LANES), F32)

    a_re = [are_ref[0, :, g * LANES:(g + 1) * LANES] for g in range(n_re)]
    a_im = [aim_ref[0, :, g * LANES:(g + 1) * LANES] for g in range(n_re)]
    sub = t_chunk // SCAN_SUBCHUNKS

    def project_in(k):
        part = slice(k * sub * SUBLANES, (k + 1) * sub * SUBLANES)
        u4 = u_ref[k * sub:(k + 1) * sub]
        u3 = jnp.concatenate([u4, jnp.zeros_like(u4)], axis=1)
        u_lo = u3.reshape(sub * SUBLANES, SLAB)
        u_hi = pltpu.roll(u3, half_rows, 1).reshape(sub * SUBLANES, SLAB)
        lhs = jnp.concatenate([u_lo, u_hi], axis=1).astype(BF16)
        bu = jnp.dot(lhs, bb_ref[0], preferred_element_type=F32)
        for g in range(n_cg):
            x_scr[g, part, :] = bu[:, g * LANES:(g + 1) * LANES]

    def scan(k, x):
        for t in range(k * sub, (k + 1) * sub):
            idx = slice(t * SUBLANES, (t + 1) * SUBLANES)
            new = [None] * n_cg
            for g in range(n_re):
                x_re, x_im = x[g], x[n_re + g]
                new[g] = a_re[g] * x_re - a_im[g] * x_im + x_scr[g, idx, :]
                new[n_re + g] = a_re[g] * x_im + a_im[g] * x_re + x_scr[n_re + g, idx, :]
            for g in range(n_cg):
                x_scr[g, idx, :] = new[g]
            x = new
        return x

    def project_out(k):
        part = slice(k * sub * SUBLANES, (k + 1) * sub * SUBLANES)
        steps = slice(k * sub, (k + 1) * sub)
        xs = jnp.concatenate([x_scr[g, part, :] for g in range(n_cg)], axis=1).astype(BF16)
        yy = jnp.dot(xs, cc_ref[0], preferred_element_type=F32)
        y_half1 = pltpu.roll(yy[:, SLAB:].reshape(sub, SUBLANES, SLAB), half_rows, 1)
        y3 = yy[:, :SLAB].reshape(sub, SUBLANES, SLAB) + y_half1
        y_ref[steps] = y3[:, 0:half_rows, :] + d_ref[...] * u_ref[steps]

    x = [carry_scr[s, g] for g in range(n_cg)]
    project_in(0)
    for k in range(SCAN_SUBCHUNKS):
        if k + 1 < SCAN_SUBCHUNKS:
            project_in(k + 1)
        x = scan(k, x)
        if k > 0:
            project_out(k - 1)
    project_out(SCAN_SUBCHUNKS - 1)
    for g in range(n_cg):
        carry_scr[s, g] = x[g]


def _ssm(u, bb, cc, a_re, a_im, d_skip, t_chunk=256):
    s, r, width = u.shape
    assert r == SUBLANES // 2
    n_slab = width // SLAB
    hs = HALF_STATES
    return pl.pallas_call(
        functools.partial(_ssm_kernel, t_chunk=t_chunk),
        grid=(s // t_chunk, n_slab),
        in_specs=[pl.BlockSpec((t_chunk, SUBLANES // 2, SLAB), lambda c, sl: (c, 0, sl)),
                  pl.BlockSpec((1, 2 * SLAB, 2 * hs), lambda c, sl: (sl, 0, 0)),
                  pl.BlockSpec((1, 2 * hs, 2 * SLAB), lambda c, sl: (sl, 0, 0)),
                  pl.BlockSpec((1, SUBLANES, hs), lambda c, sl: (sl, 0, 0)),
                  pl.BlockSpec((1, SUBLANES, hs), lambda c, sl: (sl, 0, 0)),
                  pl.BlockSpec((1, SLAB), lambda c, sl: (0, sl))],
        out_specs=pl.BlockSpec((t_chunk, SUBLANES // 2, SLAB), lambda c, sl: (c, 0, sl)),
        out_shape=jax.ShapeDtypeStruct((s, SUBLANES // 2, width), F32),
        scratch_shapes=[pltpu.VMEM((2 * hs // LANES, SUBLANES * t_chunk, LANES), F32),
                        pltpu.VMEM((n_slab, 2 * hs // LANES, SUBLANES, LANES), F32)],
        compiler_params=_params(("arbitrary", "arbitrary")),
        name="s5_scan",
    )(u, bb, cc, a_re, a_im, d_skip.reshape(1, width))


def _mix_kernel(a0_ref, a1_ref, y_ref, x_ref, mod_ref, wglu_ref, bglu_ref, ga_ref, gs_ref, wo_ref, lng_ref, lnb_ref,
                o_ref):
    aw = a0_ref.shape[1]
    a = _rms_norm_rows(jnp.concatenate([a0_ref[...], a1_ref[...]], axis=0), ga_ref[...])
    y = jax.nn.gelu(y_ref[...])
    z = y * jax.nn.sigmoid(jnp.dot(y.astype(BF16), wglu_ref[...], preferred_element_type=F32) + bglu_ref[...])
    sn = _rms_norm_rows(z, gs_ref[...])
    mix = (jnp.dot(a.astype(BF16), wo_ref[0:aw, :], preferred_element_type=F32)
           + jnp.dot(sn.astype(BF16), wo_ref[aw:, :], preferred_element_type=F32))
    gate = mod_ref[0, 5:6, :]
    o_ref[...] = _residual_layer_norm(x_ref[...], mix, 1.0 + gate, lng_ref[...], lnb_ref[...])


def _mix(attn, y_rows, x2d, mod, w_glu, b_glu, g_attn, g_ssm, w_out, ln_g, ln_b, *, seq):
    n, d = x2d.shape
    _, _, n_steps, blk, aw = attn.shape
    tm = 2 * blk
    sw = y_rows.shape[1] // (n // seq)
    tiles_per_batch = seq // tm
    const = lambda m: (0, 0)

    def attn_spec(e):
        def index(m):
            i = 2 * (m % tiles_per_batch) + e
            folded = i >= n_steps
            return (m // tiles_per_batch, folded.astype(jnp.int32), jnp.where(folded, 2 * n_steps - 1 - i, i), 0, 0)
        return pl.BlockSpec((None, None, None, blk, aw), index)

    return pl.pallas_call(
        _mix_kernel,
        grid=(n // tm,),
        in_specs=[attn_spec(0), attn_spec(1),
                  pl.BlockSpec((tm, sw), lambda m: (m % tiles_per_batch, m // tiles_per_batch)),
                  pl.BlockSpec((tm, d), lambda m: (m, 0)),
                  pl.BlockSpec((1, N_MOD, d), lambda m: (m // tiles_per_batch, 0, 0)),
                  pl.BlockSpec((sw, sw), const),
                  pl.BlockSpec((1, sw), const),
                  pl.BlockSpec((1, aw), const),
                  pl.BlockSpec((1, sw), const),
                  pl.BlockSpec((aw + sw, d), const),
                  pl.BlockSpec((1, d), const),
                  pl.BlockSpec((1, d), const)],
        out_specs=pl.BlockSpec((tm, d), lambda m: (m, 0)),
        out_shape=jax.ShapeDtypeStruct((n, d), F32),
        compiler_params=_params(("parallel",)),
        name="mix",
    )(attn, attn, y_rows, x2d, mod, w_glu, b_glu.reshape(1, sw), g_attn.reshape(1, aw), g_ssm.reshape(1, sw),
      w_out, ln_g.reshape(1, d), ln_b.reshape(1, d))


def kernel(x, c, w_ada, b_ada, ffn1_w_gate, ffn1_w_up, ffn1_w_down, ln1_g, ln1_b, w_in, attn_norm_g, ssm_lambda_re, ssm_lambda_im, ssm_log_dt, ssm_b_re, ssm_b_im, ssm_c_re, ssm_c_im, ssm_d, ssm_w_glu, ssm_b_glu, ssm_norm_g, w_out, ln2_g, ln2_b, ffn2_w_gate, ffn2_w_up, ffn2_w_down, ln3_g, ln3_b):
    b, s, d = x.shape
    assert w_ada.shape[0] == DEPTH
    slopes = 2.0 ** (-(8.0 / N_HEADS) * jnp.arange(1, N_HEADS + 1, dtype=F32))
    c_pad = jnp.pad(c, ((0, SUBLANES - b), (0, 0)))
    x2d = x.reshape(b * s, d)
    for l in range(DEPTH):
        mod = _adaln(c_pad, w_ada[l], b_ada[l])[:b].reshape(b, N_MOD, d)

        x2d, (w_in_bf, w_out_bf, w_glu_bf) = _ffn(
            x2d, mod, _col_tiles(ffn1_w_gate[l].astype(BF16), FFN_TILE), _col_tiles(ffn1_w_up[l].astype(BF16), FFN_TILE),
            ffn1_w_down[l].astype(BF16), ln1_g[l], ln1_b[l], mod_base=0, seq=s,
            cast_jobs=(w_in[l], w_out[l], ssm_w_glu[l]))

        q, k, v, u = _inproj(x2d, mod, w_in_bf, seq=s)
        aw = q.shape[1]
        attn, (w2_gate, w2_up, w2_down) = _attention(
            q.reshape(b, s, aw), k.reshape(b, s, aw), v.reshape(b, s, aw), slopes,
            cast_jobs=((ffn2_w_gate[l], FFN_TILE), (ffn2_w_up[l], FFN_TILE), ffn2_w_down[l]))

        bb, cc, a_re, a_im = _ssm_tables(ssm_lambda_re[l], ssm_lambda_im[l], ssm_log_dt[l],
                                         ssm_b_re[l], ssm_b_im[l], ssm_c_re[l], ssm_c_im[l])
        sw = u.shape[1] // b
        y = _ssm(u.reshape(s, b, sw), bb, cc, a_re, a_im, ssm_d[l])

        x2d = _mix(attn, y.reshape(s, b * sw), x2d, mod, w_glu_bf, ssm_b_glu[l],
                   attn_norm_g[l], ssm_norm_g[l], w_out_bf, ln2_g[l], ln2_b[l], seq=s)

        x2d, _ = _ffn(x2d, mod, w2_gate, w2_up, w2_down, ln3_g[l], ln3_b[l], mod_base=6, seq=s)
    return x2d.reshape(b, s, d)
```

```python
import functools
import math

import jax
import jax.numpy as jnp
from jax import lax
from jax.experimental import pallas as pl
from jax.experimental.pallas import tpu as pltpu

F32 = jnp.float32
BF16 = jnp.bfloat16

N_HEADS = 8
HEAD_DIM = 128
MOBA_BLOCK = 256
MOBA_TOPK = 3
SSM_GROUP = 16
SSM_STATE = 64
N_MOD = 9
LN_EPS = 1e-5
NEG_INF = -1e30
LOG2E = math.log2(math.e)
DEPTH = 1
ALPHA = (2.0 * DEPTH) ** 0.25
Q_PRESCALE = HEAD_DIM ** -0.5 * LOG2E

LANES = 128
SUBLANES = 8
VMEM_LIMIT_BYTES = 56 * 1024 * 1024

SLAB = LANES
HALF_STATES = 256
SCAN_SUBCHUNKS = 8
FFN_TILE = 512


def _params(sem):
    return pltpu.CompilerParams(dimension_semantics=sem, vmem_limit_bytes=VMEM_LIMIT_BYTES)


def _residual_layer_norm(x, y, coef, g, b):
    r = x + (coef * (1.0 / ALPHA)) * y
    mu = jnp.mean(r, axis=-1, keepdims=True)
    d = r - mu
    var = jnp.mean(d * d, axis=-1, keepdims=True)
    return d * lax.rsqrt(var + LN_EPS / (ALPHA * ALPHA)) * g + b


def _rms_norm_rows(y, g):
    return y * lax.rsqrt(jnp.mean(y * y, axis=-1, keepdims=True) + LN_EPS) * g


def _adaln_kernel(c_ref, w_ref, b_ref, o_ref):
    c = c_ref[...]
    c_act = (c * jax.nn.sigmoid(c)).astype(BF16)
    o_ref[...] = jnp.dot(c_act, w_ref[...].astype(BF16), preferred_element_type=F32) + b_ref[...]


def _adaln(c_pad, w_ada, b_ada, tn=1024):
    rows, d = c_pad.shape
    n = w_ada.shape[1]
    return pl.pallas_call(
        _adaln_kernel,
        grid=(n // tn,),
        in_specs=[pl.BlockSpec((rows, d), lambda j: (0, 0)),
                  pl.BlockSpec((d, tn), lambda j: (0, j)),
                  pl.BlockSpec((1, tn), lambda j: (0, j))],
        out_specs=pl.BlockSpec((rows, tn), lambda j: (0, j)),
        out_shape=jax.ShapeDtypeStruct((rows, n), F32),
        compiler_params=_params(("arbitrary",)),
        name="adaln",
    )(c_pad, w_ada, b_ada.reshape(1, n))


def _cast_side_job(cast_in, cast_out):
    for src, dst in zip(cast_in, cast_out):
        dst[...] = src[...].astype(BF16)


def _ffn_kernel(*refs, mod_base, n_cast, odd_tiles):
    x_ref, mod_ref, wga_ref, wua_ref, wda_ref, wgb_ref, wub_ref, wdb_ref, lng_ref, lnb_ref = refs[:10]
    cast_in = refs[10:10 + n_cast]
    o_ref = refs[10 + n_cast]
    cast_out = refs[11 + n_cast:11 + 2 * n_cast]
    u_scr, acc_scr = refs[11 + 2 * n_cast:]
    f = pl.program_id(1)
    last = pl.num_programs(1) - 1

    @pl.when(f == 0)
    def _():
        shift = mod_ref[0, mod_base:mod_base + 1, :]
        scale = mod_ref[0, mod_base + 1:mod_base + 2, :]
        u_scr[...] = (x_ref[...] * (1.0 + scale) + shift).astype(BF16)
        acc_scr[...] = jnp.zeros_like(acc_scr)

    def swiglu(wg_ref, wu_ref, wd_ref, rows=slice(None)):
        u = u_scr[rows, :]
        g = jnp.dot(u, wg_ref[...], preferred_element_type=F32)
        up = jnp.dot(u, wu_ref[...], preferred_element_type=F32)
        h = (g * jax.nn.sigmoid(g)) * up
        return jnp.dot(h.astype(BF16), wd_ref[...], preferred_element_type=F32)

    def both_tiles():
        y = swiglu(wga_ref, wua_ref, wda_ref)
        _cast_side_job(cast_in, cast_out)
        return y + swiglu(wgb_ref, wub_ref, wdb_ref)

    def finish(y, rows=slice(None)):
        gate = mod_ref[0, mod_base + 2:mod_base + 3, :]
        o_ref[rows, :] = _residual_layer_norm(x_ref[rows, :], acc_scr[rows, :] + y, 0.5 * (1.0 + gate),
                                              lng_ref[...], lnb_ref[...])

    @pl.when(f < last)
    def _():
        acc_scr[...] += both_tiles()

    @pl.when(f == last)
    def _():
        if odd_tiles:
            _cast_side_job(cast_in, cast_out)
            half = u_scr.shape[0] // 2
            for rows in (slice(0, half), slice(half, 2 * half)):
                finish(swiglu(wga_ref, wua_ref, wda_ref, rows), rows)
        else:
            finish(both_tiles())


def _cast_job_spec(w, grid):
    rows, cols = w.shape
    bf16_rows = 2 * SUBLANES
    steps = math.prod(grid)
    n_parts = max(k for k in range(1, steps + 1)
                  if steps % k == 0 and rows % k == 0 and (rows // k) % bf16_rows == 0)
    visits = steps // n_parts

    def index(*ids):
        step = 0
        for i, extent in zip(ids, grid):
            step = step * extent + i
        return step // visits, 0

    return pl.BlockSpec((rows // n_parts, cols), index)


def _ffn(x2d, mod, wg, wu, wd, ln_g, ln_b, *, mod_base, seq, cast_jobs=(), tm=512):
    n, d = x2d.shape
    tf = FFN_TILE
    n_tiles = wg.shape[1] // tf
    tiles_per_batch = seq // tm
    grid = (n // tm, (n_tiles + 1) // 2)
    tile_a = lambda f: 2 * f
    tile_b = lambda f: jnp.minimum(2 * f + 1, n_tiles - 1)
    cast_specs = [_cast_job_spec(w, grid) for w in cast_jobs]
    out = pl.pallas_call(
        functools.partial(_ffn_kernel, mod_base=mod_base, n_cast=len(cast_jobs), odd_tiles=n_tiles % 2 == 1),
        grid=grid,
        in_specs=[pl.BlockSpec((tm, d), lambda m, f: (m, 0)),
                  pl.BlockSpec((1, N_MOD, d), lambda m, f: (m // tiles_per_batch, 0, 0)),
                  pl.BlockSpec((d, tf), lambda m, f: (0, tile_a(f))),
                  pl.BlockSpec((d, tf), lambda m, f: (0, tile_a(f))),
                  pl.BlockSpec((tf, d), lambda m, f: (tile_a(f), 0)),
                  pl.BlockSpec((d, tf), lambda m, f: (0, tile_b(f))),
                  pl.BlockSpec((d, tf), lambda m, f: (0, tile_b(f))),
                  pl.BlockSpec((tf, d), lambda m, f: (tile_b(f), 0)),
                  pl.BlockSpec((1, d), lambda m, f: (0, 0)),
                  pl.BlockSpec((1, d), lambda m, f: (0, 0))] + cast_specs,
        out_specs=[pl.BlockSpec((tm, d), lambda m, f: (m, 0))] + cast_specs,
        out_shape=[jax.ShapeDtypeStruct((n, d), F32)] + [jax.ShapeDtypeStruct(w.shape, BF16) for w in cast_jobs],
        scratch_shapes=[pltpu.VMEM((tm, d), BF16), pltpu.VMEM((tm, d), F32)],
        compiler_params=_params(("arbitrary", "arbitrary")),
        name="ffn",
    )(x2d, mod, wg, wu, wd, wg, wu, wd, ln_g.reshape(1, d), ln_b.reshape(1, d), *cast_jobs)
    return out[0], out[1:]


def _inproj_kernel(x_ref, mod_ref, w_ref, q_ref, k_ref, v_ref, u_ref):
    width = q_ref.shape[1]
    shift = mod_ref[0, 3:4, :]
    scale = mod_ref[0, 4:5, :]
    xm = (x_ref[...] * (1.0 + scale) + shift).astype(BF16)

    def project(col):
        return jnp.dot(xm, w_ref[:, col * width:(col + 1) * width], preferred_element_type=F32)

    q_ref[...] = (project(0) * Q_PRESCALE).astype(BF16)
    k_ref[...] = project(1).astype(BF16)
    v_ref[...] = project(2).astype(BF16)
    u_ref[...] = project(3)


def _inproj(x2d, mod, w_in, *, seq, tm=512):
    n, d = x2d.shape
    width = w_in.shape[1] // 4
    tiles_per_batch = seq // tm
    n_batch = n // seq
    out_spec = pl.BlockSpec((tm, width), lambda m: (m, 0))
    u_spec = pl.BlockSpec((tm, width), lambda m: (m % tiles_per_batch, m // tiles_per_batch))
    return pl.pallas_call(
        _inproj_kernel,
        grid=(n // tm,),
        in_specs=[pl.BlockSpec((tm, d), lambda m: (m, 0)),
                  pl.BlockSpec((1, N_MOD, d), lambda m: (m // tiles_per_batch, 0, 0)),
                  pl.BlockSpec((d, 4 * width), lambda m: (0, 0), pipeline_mode=pl.Buffered(1))],
        out_specs=[out_spec, out_spec, out_spec, u_spec],
        out_shape=[jax.ShapeDtypeStruct((n, width), BF16)] * 3 + [jax.ShapeDtypeStruct((seq, n_batch * width), F32)],
        compiler_params=_params(("arbitrary",)),
        name="inproj",
    )(x2d, mod, w_in)


def _dot_nt(a, b):
    return lax.dot_general(a, b, (((1,), (1,)), ((), ())), preferred_element_type=F32)


def _attn_kernel(*refs, n_blocks, hg, n_cast):
    slopes_ref, qa_ref, qb_ref, k_ref, v_ref = refs[:5]
    cast_in = refs[5:5 + n_cast]
    o_ref = refs[5 + n_cast]
    cast_out = refs[6 + n_cast:6 + 2 * n_cast]
    (kmean_scr, vt_scr, base_scr, q_scr, pk_scr, m_scr, acc_scr,
     sa_scr, sb_scr, pa_scr, pb_scr) = refs[6 + 2 * n_cast:]
    g = pl.program_id(1)
    p = pl.program_id(2)
    blk = MOBA_BLOCK
    dh = HEAD_DIM
    slope = [slopes_ref[g * hg + h] for h in range(hg)]
    hcols = [slice(h * dh, (h + 1) * dh) for h in range(hg)]
    q_refs = (qa_ref, qb_ref)
    tile_blk = (p, n_blocks - 1 - p)

    key_id = lax.broadcasted_iota(jnp.int32, (blk, blk), 0)
    qry_id = lax.broadcasted_iota(jnp.int32, (blk, blk), 1)
    rel = (qry_id - key_id).astype(F32)

    @pl.when(p == 0)
    def _():
        row_id = lax.broadcasted_iota(jnp.int32, (vt_scr.shape[1] - dh, vt_scr.shape[2]), 0)
        ones_row = jnp.where(row_id == 0, 1.0, 0.0).astype(BF16)
        for h in range(hg):
            base_scr[h] = (-LOG2E * slope[h]) * rel
            vt_scr[h, dh:, :] = ones_row
            for j in range(n_blocks):
                rows = slice(j * blk, (j + 1) * blk)
                kmean_scr[h, j:j + 1, :] = jnp.mean(k_ref[0, rows, hcols[h]].astype(F32), axis=0, keepdims=True)
                vt_scr[h, 0:dh, rows] = v_ref[0, rows, hcols[h]].astype(F32).T.astype(BF16)

    def item(n):
        tile = (n >= p).astype(jnp.int32)
        return tile, n - p * tile, jnp.where(tile == 0, tile_blk[0], tile_blk[1])

    def scores_into(n, s_scr):
        tile, j, _ = item(n)
        keys = pl.ds(pl.multiple_of(j * blk, blk), blk)
        for h in range(hg):
            s_scr[h] = _dot_nt(k_ref[0, keys, hcols[h]], q_scr[tile, h])

    def softmax_stage(n, s_scr, p_scr):
        tile, j, i_blk = item(n)
        j_f = j.astype(F32)
        i_f = i_blk.astype(F32)
        alphas = []
        for h in range(hg):
            selected = ((pk_scr[tile, h, 0:1, :] == j_f) | (pk_scr[tile, h, 1:2, :] == j_f)
                        | (pk_scr[tile, h, 2:3, :] == j_f))
            row_bias = jnp.where(selected, (j_f - i_f) * (float(blk) * LOG2E * slope[h]), NEG_INF)
            x = s_scr[h] + base_scr[h]
            m_old = m_scr[tile, h]
            m_new = jnp.maximum(m_old, jnp.max(x, axis=0, keepdims=True) + row_bias)
            p_scr[h] = jnp.exp2(x - (m_new - row_bias)).astype(BF16)
            m_scr[tile, h] = m_new
            alphas.append(jnp.exp2(m_old - m_new))
        return tuple(alphas)

    def accumulate(n, p_scr, alphas):
        tile, j, _ = item(n)
        keys = pl.ds(pl.multiple_of(j * blk, blk), blk)
        pv = [jnp.dot(vt_scr[h, :, keys], p_scr[h], preferred_element_type=F32) for h in range(hg)]
        for h in range(hg):
            acc_scr[tile, h] = alphas[h] * acc_scr[tile, h] + pv[h]

    for t in range(2):
        for h in range(hg):
            q_scr[t, h] = q_refs[t][0, 0, :, hcols[h]]
    scores_into(0, sa_scr)
    own = [pl.ds(pl.multiple_of(tile_blk[t] * blk, blk), blk) for t in range(2)]
    gates = [[_dot_nt(kmean_scr[h].astype(BF16), q_scr[t, h]) for h in range(hg)] for t in range(2)]
    s_own = [[_dot_nt(k_ref[0, own[t], hcols[h]], q_scr[t, h]) for h in range(hg)] for t in range(2)]
    _cast_side_job(cast_in, cast_out)
    p_own = [[None] * hg for _ in range(2)]
    for t in range(2):
        i_f = tile_blk[t].astype(F32)
        for h in range(hg):
            blk_id = lax.broadcasted_iota(jnp.int32, gates[t][h].shape, 0).astype(F32)
            work = jnp.where(blk_id < i_f, gates[t][h], NEG_INF)
            for r in range(MOBA_TOPK):
                mx = jnp.max(work, axis=0, keepdims=True)
                idx = jnp.min(jnp.where(work == mx, blk_id, float(n_blocks)), axis=0, keepdims=True)
                work = jnp.where(blk_id == idx, -jnp.inf, work)
                pk_scr[t, h, r:r + 1, :] = jnp.where(idx < i_f, idx, -1.0)
            s = jnp.where(rel >= 0.0, s_own[t][h] + base_scr[h], NEG_INF)
            m = jnp.max(s, axis=0, keepdims=True)
            m_scr[t, h] = m
            p_own[t][h] = jnp.exp2(s - m).astype(BF16)
    for t in range(2):
        pv_own = [jnp.dot(vt_scr[h, :, own[t]], p_own[t][h], preferred_element_type=F32) for h in range(hg)]
        for h in range(hg):
            acc_scr[t, h] = pv_own[h]

    s_slots = (sa_scr, sb_scr)
    p_slots = (pa_scr, pb_scr)
    a_prev = None
    for n in range(n_blocks - 1):
        if n + 1 < n_blocks - 1:
            scores_into(n + 1, s_slots[(n + 1) % 2])
        a_cur = softmax_stage(n, s_slots[n % 2], p_slots[n % 2])
        if n > 0:
            accumulate(n - 1, p_slots[(n - 1) % 2], a_prev)
        a_prev = a_cur
    accumulate(n_blocks - 2, p_slots[(n_blocks - 2) % 2], a_prev)
    for t in range(2):
        for h in range(hg):
            acc = acc_scr[t, h]
            o_ref[0, t, 0, :, hcols[h]] = (acc[0:dh] / acc[dh:dh + 1]).T


def _attention(q, k, v, slopes, cast_jobs=(), hg=4):
    b, s, width = q.shape
    n_heads = width // HEAD_DIM
    n_blocks = s // MOBA_BLOCK
    assert n_blocks % 2 == 0 and n_heads % hg == 0
    n_steps = n_blocks // 2
    gw = hg * HEAD_DIM
    ones_rows = 16
    kv_spec = pl.BlockSpec((1, s, gw), lambda bi, g, p: (bi, 0, g))
    slot = lambda dims, dt: pltpu.VMEM(dims, dt)
    grid = (b, n_heads // hg, n_steps)
    cast_specs = [_cast_job_spec(w, grid) for w in cast_jobs]
    out = pl.pallas_call(
        functools.partial(_attn_kernel, n_blocks=n_blocks, hg=hg, n_cast=len(cast_jobs)),
        grid=grid,
        in_specs=[pl.BlockSpec(memory_space=pltpu.SMEM),
                  pl.BlockSpec((1, 1, MOBA_BLOCK, gw), lambda bi, g, p: (bi, p, 0, g)),
                  pl.BlockSpec((1, 1, MOBA_BLOCK, gw), lambda bi, g, p: (bi, n_blocks - 1 - p, 0, g)),
                  kv_spec, kv_spec] + cast_specs,
        out_specs=[pl.BlockSpec((1, 2, 1, MOBA_BLOCK, gw), lambda bi, g, p: (bi, 0, p, 0, g))] + cast_specs,
        out_shape=([jax.ShapeDtypeStruct((b, 2, n_steps, MOBA_BLOCK, width), F32)]
                   + [jax.ShapeDtypeStruct(w.shape, BF16) for w in cast_jobs]),
        scratch_shapes=[slot((hg, n_blocks, HEAD_DIM), F32),
                        slot((hg, HEAD_DIM + ones_rows, s), BF16),
                        slot((hg, MOBA_BLOCK, MOBA_BLOCK), F32),
                        slot((2, hg, MOBA_BLOCK, HEAD_DIM), BF16),
                        slot((2, hg, SUBLANES, MOBA_BLOCK), F32),
                        slot((2, hg, 1, MOBA_BLOCK), F32),
                        slot((2, hg, HEAD_DIM + ones_rows, MOBA_BLOCK), F32),
                        slot((hg, MOBA_BLOCK, MOBA_BLOCK), F32), slot((hg, MOBA_BLOCK, MOBA_BLOCK), F32),
                        slot((hg, MOBA_BLOCK, MOBA_BLOCK), BF16), slot((hg, MOBA_BLOCK, MOBA_BLOCK), BF16)],
        compiler_params=_params(("arbitrary", "arbitrary", "arbitrary")),
        name="moba_attn",
    )(slopes, q.reshape(b, n_blocks, MOBA_BLOCK, width), q.reshape(b, n_blocks, MOBA_BLOCK, width), k, v, *cast_jobs)
    return out[0], out[1:]


def _ssm_tables(lam_re, lam_im, log_dt, b_re, b_im, c_re, c_im):
    g_total, p = lam_re.shape
    n_slab = g_total * SSM_GROUP // SLAB
    gps = SLAB // SSM_GROUP
    gph = gps // 2
    dt = jnp.exp(log_dt)[:, None]
    decay = jnp.exp(lam_re * dt)
    ab_re = decay * jnp.cos(lam_im * dt)
    ab_im = decay * jnp.sin(lam_im * dt)
    den = lam_re * lam_re + lam_im * lam_im
    coef_re = ((ab_re - 1.0) * lam_re + ab_im * lam_im) / den
    coef_im = (ab_im * lam_re - (ab_re - 1.0) * lam_im) / den
    bb_re = coef_re[..., None] * b_re - coef_im[..., None] * b_im
    bb_im = coef_re[..., None] * b_im + coef_im[..., None] * b_re
    eye = jnp.eye(gps, dtype=F32)

    bbri = jnp.stack([bb_re, bb_im], axis=0).reshape(2, n_slab, gps, p, SSM_GROUP)
    bb = jnp.einsum('rsgph,gk->sghkrp', bbri, eye)
    bb = bb.reshape(n_slab, gps, SSM_GROUP, 2, gph, 2, p).transpose(0, 1, 2, 3, 5, 4, 6)
    bb = bb.reshape(n_slab, SLAB, 2, 2 * HALF_STATES).transpose(0, 2, 1, 3).reshape(n_slab, 2 * SLAB, 2 * HALF_STATES)

    ccri = jnp.stack([c_re, -c_im], axis=0).reshape(2, n_slab, gps, SSM_GROUP, p)
    cc = jnp.einsum('rsghp,gk->skrpgh', ccri, eye)
    cc = cc.reshape(n_slab, 2, gph, 2, p, gps, SSM_GROUP).transpose(0, 1, 3, 2, 4, 5, 6)
    cc = cc.reshape(n_slab, 2, 2 * HALF_STATES, SLAB).transpose(0, 2, 1, 3).reshape(n_slab, 2 * HALF_STATES, 2 * SLAB)

    def rows(a):
        a = a.reshape(n_slab, 2, 1, HALF_STATES)
        return jnp.broadcast_to(a, (n_slab, 2, 4, HALF_STATES)).reshape(n_slab, SUBLANES, HALF_STATES)

    return bb.astype(BF16), cc.astype(BF16), rows(ab_re), rows(ab_im)


def _ssm_kernel(u_ref, bb_ref, cc_ref, are_ref, aim_ref, d_ref, y_ref, x_scr, carry_scr, *, t_chunk):
    c = pl.program_id(0)
    s = pl.program_id(1)
    hs = HALF_STATES
    n_cg = 2 * hs // LANES
    n_re = hs // LANES
    half_rows = SUBLANES // 2


    @pl.when(c == 0)
    def _():
        carry_scr[s] = jnp.zeros((n_cg, SUBLANES, LANES), F32)

    a_re = [are_ref[0, :, g * LANES:(g + 1) * LANES] for g in range(n_re)]
    a_im = [aim_ref[0, :, g * LANES:(g + 1) * LANES] for g in range(n_re)]
    sub = t_chunk // SCAN_SUBCHUNKS

    def project_in(k):
        part = slice(k * sub * SUBLANES, (k + 1) * sub * SUBLANES)
        u4 = u_ref[k * sub:(k + 1) * sub]
        u3 = jnp.concatenate([u4, jnp.zeros_like(u4)], axis=1)
        u_lo = u3.reshape(sub * SUBLANES, SLAB)
        u_hi = pltpu.roll(u3, half_rows, 1).reshape(sub * SUBLANES, SLAB)
        lhs = jnp.concatenate([u_lo, u_hi], axis=1).astype(BF16)
        bu = jnp.dot(lhs, bb_ref[0], preferred_element_type=F32)
        for g in range(n_cg):
            x_scr[g, part, :] = bu[:, g * LANES:(g + 1) * LANES]

    def scan(k, x):
        for t in range(k * sub, (k + 1) * sub):
            idx = slice(t * SUBLANES, (t + 1) * SUBLANES)
            new = [None] * n_cg
            for g in range(n_re):
                x_re, x_im = x[g], x[n_re + g]
                new[g] = a_re[g] * x_re - a_im[g] * x_im + x_scr[g, idx, :]
                new[n_re + g] = a_re[g] * x_im + a_im[g] * x_re + x_scr[n_re + g, idx, :]
            for g in range(n_cg):
                x_scr[g, idx, :] = new[g]
            x = new
        return x

    def project_out(k):
        part = slice(k * sub * SUBLANES, (k + 1) * sub * SUBLANES)
        steps = slice(k * sub, (k + 1) * sub)
        xs = jnp.concatenate([x_scr[g, part, :] for g in range(n_cg)], axis=1).astype(BF16)
        yy = jnp.dot(xs, cc_ref[0], preferred_element_type=F32)
        y_half1 = pltpu.roll(yy[:, SLAB:].reshape(sub, SUBLANES, SLAB), half_rows, 1)
        y3 = yy[:, :SLAB].reshape(sub, SUBLANES, SLAB) + y_half1
        y_ref[steps] = y3[:, 0:half_rows, :] + d_ref[...] * u_ref[steps]

    x = [carry_scr[s, g] for g in range(n_cg)]
    project_in(0)
    for k in range(SCAN_SUBCHUNKS):
        if k + 1 < SCAN_SUBCHUNKS:
            project_in(k + 1)
        x = scan(k, x)
        if k > 0:
            project_out(k - 1)
    project_out(SCAN_SUBCHUNKS - 1)
    for g in range(n_cg):
        carry_scr[s, g] = x[g]


def _ssm(u, bb, cc, a_re, a_im, d_skip, t_chunk=512):
    s, r, width = u.shape
    assert r == SUBLANES // 2
    n_slab = width // SLAB
    hs = HALF_STATES
    return pl.pallas_call(
        functools.partial(_ssm_kernel, t_chunk=t_chunk),
        grid=(s // t_chunk, n_slab),
        in_specs=[pl.BlockSpec((t_chunk, SUBLANES // 2, SLAB), lambda c, sl: (c, 0, sl)),
                  pl.BlockSpec((1, 2 * SLAB, 2 * hs), lambda c, sl: (sl, 0, 0)),
                  pl.BlockSpec((1, 2 * hs, 2 * SLAB), lambda c, sl: (sl, 0, 0)),
                  pl.BlockSpec((1, SUBLANES, hs), lambda c, sl: (sl, 0, 0)),
                  pl.BlockSpec((1, SUBLANES, hs), lambda c, sl: (sl, 0, 0)),
                  pl.BlockSpec((1, SLAB), lambda c, sl: (0, sl))],
        out_specs=pl.BlockSpec((t_chunk, SUBLANES // 2, SLAB), lambda c, sl: (c, 0, sl)),
        out_shape=jax.ShapeDtypeStruct((s, SUBLANES // 2, width), F32),
        scratch_shapes=[pltpu.VMEM((2 * hs // LANES, SUBLANES * t_chunk, LANES), F32),
                        pltpu.VMEM((n_slab, 2 * hs // LANES, SUBLANES, LANES), F32)],
        compiler_params=_params(("arbitrary", "arbitrary")),
        name="s5_scan",
    )(u, bb, cc, a_re, a_im, d_skip.reshape(1, width))


def _mix_kernel(a0_ref, a1_ref, y_ref, x_ref, mod_ref, wglu_ref, bglu_ref, ga_ref, gs_ref, wo_ref, lng_ref, lnb_ref,
                o_ref):
    aw = a0_ref.shape[1]
    a = _rms_norm_rows(jnp.concatenate([a0_ref[...], a1_ref[...]], axis=0), ga_ref[...])
    y = jax.nn.gelu(y_ref[...])
    z = y * jax.nn.sigmoid(jnp.dot(y.astype(BF16), wglu_ref[...], preferred_element_type=F32) + bglu_ref[...])
    sn = _rms_norm_rows(z, gs_ref[...])
    mix = (jnp.dot(a.astype(BF16), wo_ref[0:aw, :], preferred_element_type=F32)
           + jnp.dot(sn.astype(BF16), wo_ref[aw:, :], preferred_element_type=F32))
    gate = mod_ref[0, 5:6, :]
    o_ref[...] = _residual_layer_norm(x_ref[...], mix, 1.0 + gate, lng_ref[...], lnb_ref[...])


def _mix(attn, y_rows, x2d, mod, w_glu, b_glu, g_attn, g_ssm, w_out, ln_g, ln_b, *, seq):
    n, d = x2d.shape
    _, _, n_steps, blk, aw = attn.shape
    tm = 2 * blk
    sw = y_rows.shape[1] // (n // seq)
    tiles_per_batch = seq // tm
    const = lambda m: (0, 0)

    def attn_spec(e):
        def index(m):
            i = 2 * (m % tiles_per_batch) + e
            folded = i >= n_steps
            return (m // tiles_per_batch, folded.astype(jnp.int32), jnp.where(folded, 2 * n_steps - 1 - i, i), 0, 0)
        return pl.BlockSpec((None, None, None, blk, aw), index)

    return pl.pallas_call(
        _mix_kernel,
        grid=(n // tm,),
        in_specs=[attn_spec(0), attn_spec(1),
                  pl.BlockSpec((tm, sw), lambda m: (m % tiles_per_batch, m // tiles_per_batch)),
                  pl.BlockSpec((tm, d), lambda m: (m, 0)),
                  pl.BlockSpec((1, N_MOD, d), lambda m: (m // tiles_per_batch, 0, 0)),
                  pl.BlockSpec((sw, sw), const),
                  pl.BlockSpec((1, sw), const),
                  pl.BlockSpec((1, aw), const),
                  pl.BlockSpec((1, sw), const),
                  pl.BlockSpec((aw + sw, d), const),
                  pl.BlockSpec((1, d), const),
                  pl.BlockSpec((1, d), const)],
        out_specs=pl.BlockSpec((tm, d), lambda m: (m, 0)),
        out_shape=jax.ShapeDtypeStruct((n, d), F32),
        compiler_params=_params(("parallel",)),
        name="mix",
    )(attn, attn, y_rows, x2d, mod, w_glu, b_glu.reshape(1, sw), g_attn.reshape(1, aw), g_ssm.reshape(1, sw),
      w_out, ln_g.reshape(1, d), ln_b.reshape(1, d))


def kernel(x, c, w_ada, b_ada, ffn1_w_gate, ffn1_w_up, ffn1_w_down, ln1_g, ln1_b, w_in, attn_norm_g, ssm_lambda_re, ssm_lambda_im, ssm_log_dt, ssm_b_re, ssm_b_im, ssm_c_re, ssm_c_im, ssm_d, ssm_w_glu, ssm_b_glu, ssm_norm_g, w_out, ln2_g, ln2_b, ffn2_w_gate, ffn2_w_up, ffn2_w_down, ln3_g, ln3_b):
    b, s, d = x.shape
    assert w_ada.shape[0] == DEPTH
    slopes = 2.0 ** (-(8.0 / N_HEADS) * jnp.arange(1, N_HEADS + 1, dtype=F32))
    c_pad = jnp.pad(c, ((0, SUBLANES - b), (0, 0)))
    x2d = x.reshape(b * s, d)
    for l in range(DEPTH):
        mod = _adaln(c_pad, w_ada[l], b_ada[l])[:b].reshape(b, N_MOD, d)

        x2d, (w_in_bf, w_out_bf, w_glu_bf) = _ffn(
            x2d, mod, ffn1_w_gate[l].astype(BF16), ffn1_w_up[l].astype(BF16), ffn1_w_down[l].astype(BF16),
            ln1_g[l], ln1_b[l], mod_base=0, seq=s, cast_jobs=(w_in[l], w_out[l], ssm_w_glu[l]))

        q, k, v, u = _inproj(x2d, mod, w_in_bf, seq=s)
        aw = q.shape[1]
        attn, (w2_gate, w2_up, w2_down) = _attention(
            q.reshape(b, s, aw), k.reshape(b, s, aw), v.reshape(b, s, aw), slopes,
            cast_jobs=(ffn2_w_gate[l], ffn2_w_up[l], ffn2_w_down[l]))

        bb, cc, a_re, a_im = _ssm_tables(ssm_lambda_re[l], ssm_lambda_im[l], ssm_log_dt[l],
                                         ssm_b_re[l], ssm_b_im[l], ssm_c_re[l], ssm_c_im[l])
        sw = u.shape[1] // b
        y = _ssm(u.reshape(s, b, sw), bb, cc, a_re, a_im, ssm_d[l])

        x2d = _mix(attn, y.reshape(s, b * sw), x2d, mod, w_glu_bf, ssm_b_glu[l],
                   attn_norm_g[l], ssm_norm_g[l], w_out_bf, ln2_g[l], ln2_b[l], seq=s)

        x2d, _ = _ffn(x2d, mod, w2_gate, w2_up, w2_down, ln3_g[l], ln3_b[l], mod_base=6, seq=s)
    return x2d.reshape(b, s, d)
```

```python
import functools
import math

import jax
import jax.numpy as jnp
from jax import lax
from jax.experimental import pallas as pl
from jax.experimental.pallas import tpu as pltpu

F32 = jnp.float32
BF16 = jnp.bfloat16

N_HEADS = 8
HEAD_DIM = 128
MOBA_BLOCK = 256
MOBA_TOPK = 3
SSM_GROUP = 16
SSM_STATE = 64
N_MOD = 9
LN_EPS = 1e-5
NEG_INF = -1e30
LOG2E = math.log2(math.e)
DEPTH = 1
ALPHA = (2.0 * DEPTH) ** 0.25
Q_PRESCALE = HEAD_DIM ** -0.5 * LOG2E

LANES = 128
SUBLANES = 8
VMEM_LIMIT_BYTES = 56 * 1024 * 1024

SLAB = LANES
HALF_STATES = 256
SCAN_SUBCHUNKS = 8
FFN_TILE = 512


def _params(sem):
    return pltpu.CompilerParams(dimension_semantics=sem, vmem_limit_bytes=VMEM_LIMIT_BYTES)


def _residual_layer_norm(x, y, coef, g, b):
    r = x + (coef * (1.0 / ALPHA)) * y
    mu = jnp.mean(r, axis=-1, keepdims=True)
    d = r - mu
    var = jnp.mean(d * d, axis=-1, keepdims=True)
    return d * lax.rsqrt(var + LN_EPS / (ALPHA * ALPHA)) * g + b


def _rms_norm_rows(y, g):
    return y * lax.rsqrt(jnp.mean(y * y, axis=-1, keepdims=True) + LN_EPS) * g


def _adaln_kernel(c_ref, w_ref, b_ref, o_ref):
    c = c_ref[...]
    c_act = (c * jax.nn.sigmoid(c)).astype(BF16)
    o_ref[...] = jnp.dot(c_act, w_ref[...].astype(BF16), preferred_element_type=F32) + b_ref[...]


def _adaln(c_pad, w_ada, b_ada, tn=1024):
    rows, d = c_pad.shape
    n = w_ada.shape[1]
    return pl.pallas_call(
        _adaln_kernel,
        grid=(n // tn,),
        in_specs=[pl.BlockSpec((rows, d), lambda j: (0, 0)),
                  pl.BlockSpec((d, tn), lambda j: (0, j)),
                  pl.BlockSpec((1, tn), lambda j: (0, j))],
        out_specs=pl.BlockSpec((rows, tn), lambda j: (0, j)),
        out_shape=jax.ShapeDtypeStruct((rows, n), F32),
        compiler_params=_params(("arbitrary",)),
        name="adaln",
    )(c_pad, w_ada, b_ada.reshape(1, n))


def _cast_side_job(cast_in, cast_out):
    for src, dst in zip(cast_in, cast_out):
        dst[...] = src[...].astype(BF16)


def _ffn_kernel(*refs, mod_base, n_cast, odd_tiles):
    x_ref, mod_ref, wga_ref, wua_ref, wda_ref, wgb_ref, wub_ref, wdb_ref, lng_ref, lnb_ref = refs[:10]
    cast_in = refs[10:10 + n_cast]
    o_ref = refs[10 + n_cast]
    cast_out = refs[11 + n_cast:11 + 2 * n_cast]
    u_scr, acc_scr = refs[11 + 2 * n_cast:]
    f = pl.program_id(1)
    last = pl.num_programs(1) - 1

    @pl.when(f == 0)
    def _():
        shift = mod_ref[0, mod_base:mod_base + 1, :]
        scale = mod_ref[0, mod_base + 1:mod_base + 2, :]
        u_scr[...] = (x_ref[...] * (1.0 + scale) + shift).astype(BF16)
        acc_scr[...] = jnp.zeros_like(acc_scr)

    def swiglu(wg_ref, wu_ref, wd_ref):
        u = u_scr[...]
        g = jnp.dot(u, wg_ref[...], preferred_element_type=F32)
        up = jnp.dot(u, wu_ref[...], preferred_element_type=F32)
        h = (g * jax.nn.sigmoid(g)) * up
        return jnp.dot(h.astype(BF16), wd_ref[...], preferred_element_type=F32)

    def both_tiles():
        y = swiglu(wga_ref, wua_ref, wda_ref)
        _cast_side_job(cast_in, cast_out)
        return y + swiglu(wgb_ref, wub_ref, wdb_ref)

    def finish(y):
        gate = mod_ref[0, mod_base + 2:mod_base + 3, :]
        o_ref[...] = _residual_layer_norm(x_ref[...], acc_scr[...] + y, 0.5 * (1.0 + gate), lng_ref[...], lnb_ref[...])

    @pl.when(f < last)
    def _():
        acc_scr[...] += both_tiles()

    @pl.when(f == last)
    def _():
        if odd_tiles:
            _cast_side_job(cast_in, cast_out)
            finish(swiglu(wga_ref, wua_ref, wda_ref))
        else:
            finish(both_tiles())


def _cast_job_spec(w, grid):
    rows, cols = w.shape
    bf16_rows = 2 * SUBLANES
    steps = math.prod(grid)
    n_parts = max(k for k in range(1, steps + 1)
                  if steps % k == 0 and rows % k == 0 and (rows // k) % bf16_rows == 0)
    visits = steps // n_parts

    def index(*ids):
        step = 0
        for i, extent in zip(ids, grid):
            step = step * extent + i
        return step // visits, 0

    return pl.BlockSpec((rows // n_parts, cols), index)


def _ffn(x2d, mod, wg, wu, wd, ln_g, ln_b, *, mod_base, seq, cast_jobs=(), tm=512):
    n, d = x2d.shape
    tf = FFN_TILE
    n_tiles = wg.shape[1] // tf
    tiles_per_batch = seq // tm
    grid = (n // tm, (n_tiles + 1) // 2)
    tile_a = lambda f: 2 * f
    tile_b = lambda f: jnp.minimum(2 * f + 1, n_tiles - 1)
    cast_specs = [_cast_job_spec(w, grid) for w in cast_jobs]
    out = pl.pallas_call(
        functools.partial(_ffn_kernel, mod_base=mod_base, n_cast=len(cast_jobs), odd_tiles=n_tiles % 2 == 1),
        grid=grid,
        in_specs=[pl.BlockSpec((tm, d), lambda m, f: (m, 0)),
                  pl.BlockSpec((1, N_MOD, d), lambda m, f: (m // tiles_per_batch, 0, 0)),
                  pl.BlockSpec((d, tf), lambda m, f: (0, tile_a(f))),
                  pl.BlockSpec((d, tf), lambda m, f: (0, tile_a(f))),
                  pl.BlockSpec((tf, d), lambda m, f: (tile_a(f), 0)),
                  pl.BlockSpec((d, tf), lambda m, f: (0, tile_b(f))),
                  pl.BlockSpec((d, tf), lambda m, f: (0, tile_b(f))),
                  pl.BlockSpec((tf, d), lambda m, f: (tile_b(f), 0)),
                  pl.BlockSpec((1, d), lambda m, f: (0, 0)),
                  pl.BlockSpec((1, d), lambda m, f: (0, 0))] + cast_specs,
        out_specs=[pl.BlockSpec((tm, d), lambda m, f: (m, 0))] + cast_specs,
        out_shape=[jax.ShapeDtypeStruct((n, d), F32)] + [jax.ShapeDtypeStruct(w.shape, BF16) for w in cast_jobs],
        scratch_shapes=[pltpu.VMEM((tm, d), BF16), pltpu.VMEM((tm, d), F32)],
        compiler_params=_params(("arbitrary", "arbitrary")),
        name="ffn",
    )(x2d, mod, wg, wu, wd, wg, wu, wd, ln_g.reshape(1, d), ln_b.reshape(1, d), *cast_jobs)
    return out[0], out[1:]


def _inproj_kernel(x_ref, mod_ref, w_ref, q_ref, k_ref, v_ref, u_ref):
    n_batch, tt, width = q_ref.shape
    xm = jnp.concatenate(
        [(x_ref[b] * (1.0 + mod_ref[b, 4:5, :]) + mod_ref[b, 3:4, :]).astype(BF16) for b in range(n_batch)], axis=0)

    def project(col):
        return jnp.dot(xm, w_ref[:, col * width:(col + 1) * width], preferred_element_type=F32)

    q_ref[...] = (project(0) * Q_PRESCALE).astype(BF16).reshape(n_batch, tt, width)
    k_ref[...] = project(1).astype(BF16).reshape(n_batch, tt, width)
    v_ref[...] = project(2).astype(BF16).reshape(n_batch, tt, width)
    u = project(3)
    for b in range(n_batch):
        u_ref[:, b, :] = u[b * tt:(b + 1) * tt, :]


def _inproj(x, mod, w_in, tt=128):
    n_batch, seq, d = x.shape
    width = w_in.shape[1] // 4
    qkv_spec = pl.BlockSpec((n_batch, tt, width), lambda m: (0, m, 0))
    return pl.pallas_call(
        _inproj_kernel,
        grid=(seq // tt,),
        in_specs=[pl.BlockSpec((n_batch, tt, d), lambda m: (0, m, 0)),
                  pl.BlockSpec((n_batch, N_MOD, d), lambda m: (0, 0, 0)),
                  pl.BlockSpec((d, 4 * width), lambda m: (0, 0), pipeline_mode=pl.Buffered(1))],
        out_specs=[qkv_spec, qkv_spec, qkv_spec, pl.BlockSpec((tt, n_batch, width), lambda m: (m, 0, 0))],
        out_shape=[jax.ShapeDtypeStruct((n_batch, seq, width), BF16)] * 3
                  + [jax.ShapeDtypeStruct((seq, n_batch, width), F32)],
        compiler_params=_params(("arbitrary",)),
        name="inproj",
    )(x, mod, w_in)


def _dot_nt(a, b):
    return lax.dot_general(a, b, (((1,), (1,)), ((), ())), preferred_element_type=F32)


def _attn_kernel(*refs, n_blocks, hg, n_cast):
    slopes_ref, qa_ref, qb_ref, k_ref, v_ref = refs[:5]
    cast_in = refs[5:5 + n_cast]
    o_ref = refs[5 + n_cast]
    cast_out = refs[6 + n_cast:6 + 2 * n_cast]
    (kmean_scr, vt_scr, base_scr, q_scr, pk_scr, m_scr, acc_scr,
     sa_scr, sb_scr, pa_scr, pb_scr) = refs[6 + 2 * n_cast:]
    g = pl.program_id(1)
    p = pl.program_id(2)
    blk = MOBA_BLOCK
    dh = HEAD_DIM
    slope = [slopes_ref[g * hg + h] for h in range(hg)]
    hcols = [slice(h * dh, (h + 1) * dh) for h in range(hg)]
    q_refs = (qa_ref, qb_ref)
    tile_blk = (p, n_blocks - 1 - p)

    key_id = lax.broadcasted_iota(jnp.int32, (blk, blk), 0)
    qry_id = lax.broadcasted_iota(jnp.int32, (blk, blk), 1)
    rel = (qry_id - key_id).astype(F32)

    @pl.when(p == 0)
    def _():
        row_id = lax.broadcasted_iota(jnp.int32, (vt_scr.shape[1] - dh, vt_scr.shape[2]), 0)
        ones_row = jnp.where(row_id == 0, 1.0, 0.0).astype(BF16)
        for h in range(hg):
            base_scr[h] = (-LOG2E * slope[h]) * rel
            vt_scr[h, dh:, :] = ones_row
            for j in range(n_blocks):
                rows = slice(j * blk, (j + 1) * blk)
                kmean_scr[h, j:j + 1, :] = jnp.mean(k_ref[0, rows, hcols[h]].astype(F32), axis=0, keepdims=True)
                vt_scr[h, 0:dh, rows] = v_ref[0, rows, hcols[h]].astype(F32).T.astype(BF16)

    def item(n):
        tile = (n >= p).astype(jnp.int32)
        return tile, n - p * tile, jnp.where(tile == 0, tile_blk[0], tile_blk[1])

    def scores_into(n, s_scr):
        tile, j, _ = item(n)
        keys = pl.ds(pl.multiple_of(j * blk, blk), blk)
        for h in range(hg):
            s_scr[h] = _dot_nt(k_ref[0, keys, hcols[h]], q_scr[tile, h])

    def softmax_stage(n, s_scr, p_scr):
        tile, j, i_blk = item(n)
        j_f = j.astype(F32)
        i_f = i_blk.astype(F32)
        alphas = []
        for h in range(hg):
            selected = ((pk_scr[tile, h, 0:1, :] == j_f) | (pk_scr[tile, h, 1:2, :] == j_f)
                        | (pk_scr[tile, h, 2:3, :] == j_f))
            row_bias = jnp.where(selected, (j_f - i_f) * (float(blk) * LOG2E * slope[h]), NEG_INF)
            x = s_scr[h] + base_scr[h]
            m_old = m_scr[tile, h]
            m_new = jnp.maximum(m_old, jnp.max(x, axis=0, keepdims=True) + row_bias)
            p_scr[h] = jnp.exp2(x - (m_new - row_bias)).astype(BF16)
            m_scr[tile, h] = m_new
            alphas.append(jnp.exp2(m_old - m_new))
        return tuple(alphas)

    def accumulate(n, p_scr, alphas):
        tile, j, _ = item(n)
        keys = pl.ds(pl.multiple_of(j * blk, blk), blk)
        pv = [jnp.dot(vt_scr[h, :, keys], p_scr[h], preferred_element_type=F32) for h in range(hg)]
        for h in range(hg):
            acc_scr[tile, h] = alphas[h] * acc_scr[tile, h] + pv[h]

    for t in range(2):
        for h in range(hg):
            q_scr[t, h] = q_refs[t][0, 0, :, hcols[h]]
    scores_into(0, sa_scr)
    own = [pl.ds(pl.multiple_of(tile_blk[t] * blk, blk), blk) for t in range(2)]
    gates = [[_dot_nt(kmean_scr[h].astype(BF16), q_scr[t, h]) for h in range(hg)] for t in range(2)]
    s_own = [[_dot_nt(k_ref[0, own[t], hcols[h]], q_scr[t, h]) for h in range(hg)] for t in range(2)]
    _cast_side_job(cast_in, cast_out)
    p_own = [[None] * hg for _ in range(2)]
    for t in range(2):
        i_f = tile_blk[t].astype(F32)
        for h in range(hg):
            blk_id = lax.broadcasted_iota(jnp.int32, gates[t][h].shape, 0).astype(F32)
            work = jnp.where(blk_id < i_f, gates[t][h], NEG_INF)
            for r in range(MOBA_TOPK):
                mx = jnp.max(work, axis=0, keepdims=True)
                idx = jnp.min(jnp.where(work == mx, blk_id, float(n_blocks)), axis=0, keepdims=True)
                work = jnp.where(blk_id == idx, -jnp.inf, work)
                pk_scr[t, h, r:r + 1, :] = jnp.where(idx < i_f, idx, -1.0)
            s = jnp.where(rel >= 0.0, s_own[t][h] + base_scr[h], NEG_INF)
            m = jnp.max(s, axis=0, keepdims=True)
            m_scr[t, h] = m
            p_own[t][h] = jnp.exp2(s - m).astype(BF16)
    for t in range(2):
        pv_own = [jnp.dot(vt_scr[h, :, own[t]], p_own[t][h], preferred_element_type=F32) for h in range(hg)]
        for h in range(hg):
            acc_scr[t, h] = pv_own[h]

    s_slots = (sa_scr, sb_scr)
    p_slots = (pa_scr, pb_scr)
    a_prev = None
    for n in range(n_blocks - 1):
        if n + 1 < n_blocks - 1:
            scores_into(n + 1, s_slots[(n + 1) % 2])
        a_cur = softmax_stage(n, s_slots[n % 2], p_slots[n % 2])
        if n > 0:
            accumulate(n - 1, p_slots[(n - 1) % 2], a_prev)
        a_prev = a_cur
    accumulate(n_blocks - 2, p_slots[(n_blocks - 2) % 2], a_prev)
    for t in range(2):
        for h in range(hg):
            acc = acc_scr[t, h]
            o_ref[0, t, 0, :, hcols[h]] = (acc[0:dh] / acc[dh:dh + 1]).T


def _attention(q, k, v, slopes, cast_jobs=(), hg=4):
    b, s, width = q.shape
    n_heads = width // HEAD_DIM
    n_blocks = s // MOBA_BLOCK
    assert n_blocks % 2 == 0 and n_heads % hg == 0
    n_steps = n_blocks // 2
    gw = hg * HEAD_DIM
    ones_rows = 16
    kv_spec = pl.BlockSpec((1, s, gw), lambda bi, g, p: (bi, 0, g))
    slot = lambda dims, dt: pltpu.VMEM(dims, dt)
    grid = (b, n_heads // hg, n_steps)
    cast_specs = [_cast_job_spec(w, grid) for w in cast_jobs]
    out = pl.pallas_call(
        functools.partial(_attn_kernel, n_blocks=n_blocks, hg=hg, n_cast=len(cast_jobs)),
        grid=grid,
        in_specs=[pl.BlockSpec(memory_space=pltpu.SMEM),
                  pl.BlockSpec((1, 1, MOBA_BLOCK, gw), lambda bi, g, p: (bi, p, 0, g)),
                  pl.BlockSpec((1, 1, MOBA_BLOCK, gw), lambda bi, g, p: (bi, n_blocks - 1 - p, 0, g)),
                  kv_spec, kv_spec] + cast_specs,
        out_specs=[pl.BlockSpec((1, 2, 1, MOBA_BLOCK, gw), lambda bi, g, p: (bi, 0, p, 0, g))] + cast_specs,
        out_shape=([jax.ShapeDtypeStruct((b, 2, n_steps, MOBA_BLOCK, width), F32)]
                   + [jax.ShapeDtypeStruct(w.shape, BF16) for w in cast_jobs]),
        scratch_shapes=[slot((hg, n_blocks, HEAD_DIM), F32),
                        slot((hg, HEAD_DIM + ones_rows, s), BF16),
                        slot((hg, MOBA_BLOCK, MOBA_BLOCK), F32),
                        slot((2, hg, MOBA_BLOCK, HEAD_DIM), BF16),
                        slot((2, hg, SUBLANES, MOBA_BLOCK), F32),
                        slot((2, hg, 1, MOBA_BLOCK), F32),
                        slot((2, hg, HEAD_DIM + ones_rows, MOBA_BLOCK), F32),
                        slot((hg, MOBA_BLOCK, MOBA_BLOCK), F32), slot((hg, MOBA_BLOCK, MOBA_BLOCK), F32),
                        slot((hg, MOBA_BLOCK, MOBA_BLOCK), BF16), slot((hg, MOBA_BLOCK, MOBA_BLOCK), BF16)],
        compiler_params=_params(("arbitrary", "arbitrary", "arbitrary")),
        name="moba_attn",
    )(slopes, q.reshape(b, n_blocks, MOBA_BLOCK, width), q.reshape(b, n_blocks, MOBA_BLOCK, width), k, v, *cast_jobs)
    return out[0], out[1:]


def _ssm_tables(lam_re, lam_im, log_dt, b_re, b_im, c_re, c_im):
    g_total, p = lam_re.shape
    n_slab = g_total * SSM_GROUP // SLAB
    gps = SLAB // SSM_GROUP
    gph = gps // 2
    dt = jnp.exp(log_dt)[:, None]
    decay = jnp.exp(lam_re * dt)
    ab_re = decay * jnp.cos(lam_im * dt)
    ab_im = decay * jnp.sin(lam_im * dt)
    den = lam_re * lam_re + lam_im * lam_im
    coef_re = ((ab_re - 1.0) * lam_re + ab_im * lam_im) / den
    coef_im = (ab_im * lam_re - (ab_re - 1.0) * lam_im) / den
    bb_re = coef_re[..., None] * b_re - coef_im[..., None] * b_im
    bb_im = coef_re[..., None] * b_im + coef_im[..., None] * b_re
    eye = jnp.eye(gps, dtype=F32)

    bbri = jnp.stack([bb_re, bb_im], axis=0).reshape(2, n_slab, gps, p, SSM_GROUP)
    bb = jnp.einsum('rsgph,gk->sghkrp', bbri, eye)
    bb = bb.reshape(n_slab, gps, SSM_GROUP, 2, gph, 2, p).transpose(0, 1, 2, 3, 5, 4, 6)
    bb = bb.reshape(n_slab, SLAB, 2, 2 * HALF_STATES).transpose(0, 2, 1, 3).reshape(n_slab, 2 * SLAB, 2 * HALF_STATES)

    ccri = jnp.stack([c_re, -c_im], axis=0).reshape(2, n_slab, gps, SSM_GROUP, p)
    cc = jnp.einsum('rsghp,gk->skrpgh', ccri, eye)
    cc = cc.reshape(n_slab, 2, gph, 2, p, gps, SSM_GROUP).transpose(0, 1, 3, 2, 4, 5, 6)
    cc = cc.reshape(n_slab, 2, 2 * HALF_STATES, SLAB).transpose(0, 2, 1, 3).reshape(n_slab, 2 * HALF_STATES, 2 * SLAB)

    def rows(a):
        a = a.reshape(n_slab, 2, 1, HALF_STATES)
        return jnp.broadcast_to(a, (n_slab, 2, 4, HALF_STATES)).reshape(n_slab, SUBLANES, HALF_STATES)

    return bb.astype(BF16), cc.astype(BF16), rows(ab_re), rows(ab_im)


def _ssm_kernel(u_ref, bb_ref, cc_ref, are_ref, aim_ref, d_ref, y_ref, x_scr, carry_scr, *, t_chunk):
    c = pl.program_id(0)
    s = pl.program_id(1)
    hs = HALF_STATES
    n_cg = 2 * hs // LANES
    n_re = hs // LANES
    half_rows = SUBLANES // 2


    @pl.when(c == 0)
    def _():
        carry_scr[s] = jnp.zeros((n_cg, SUBLANES, LANES), F32)

    a_re = [are_ref[0, :, g * LANES:(g + 1) * LANES] for g in range(n_re)]
    a_im = [aim_ref[0, :, g * LANES:(g + 1) * LANES] for g in range(n_re)]
    sub = t_chunk // SCAN_SUBCHUNKS

    def project_in(k):
        part = slice(k * sub * SUBLANES, (k + 1) * sub * SUBLANES)
        u4 = u_ref[k * sub:(k + 1) * sub]
        u3 = jnp.concatenate([u4, jnp.zeros_like(u4)], axis=1)
        u_lo = u3.reshape(sub * SUBLANES, SLAB)
        u_hi = pltpu.roll(u3, half_rows, 1).reshape(sub * SUBLANES, SLAB)
        lhs = jnp.concatenate([u_lo, u_hi], axis=1).astype(BF16)
        bu = jnp.dot(lhs, bb_ref[0], preferred_element_type=F32)
        for g in range(n_cg):
            x_scr[g, part, :] = bu[:, g * LANES:(g + 1) * LANES]

    def scan(k, x):
        for t in range(k * sub, (k + 1) * sub):
            idx = slice(t * SUBLANES, (t + 1) * SUBLANES)
            new = [None] * n_cg
            for g in range(n_re):
                x_re, x_im = x[g], x[n_re + g]
                new[g] = a_re[g] * x_re - a_im[g] * x_im + x_scr[g, idx, :]
                new[n_re + g] = a_re[g] * x_im + a_im[g] * x_re + x_scr[n_re + g, idx, :]
            for g in range(n_cg):
                x_scr[g, idx, :] = new[g]
            x = new
        return x

    def project_out(k):
        part = slice(k * sub * SUBLANES, (k + 1) * sub * SUBLANES)
        steps = slice(k * sub, (k + 1) * sub)
        xs = jnp.concatenate([x_scr[g, part, :] for g in range(n_cg)], axis=1).astype(BF16)
        yy = jnp.dot(xs, cc_ref[0], preferred_element_type=F32)
        y_half1 = pltpu.roll(yy[:, SLAB:].reshape(sub, SUBLANES, SLAB), half_rows, 1)
        y3 = yy[:, :SLAB].reshape(sub, SUBLANES, SLAB) + y_half1
        y_ref[steps] = y3[:, 0:half_rows, :] + d_ref[...] * u_ref[steps]

    x = [carry_scr[s, g] for g in range(n_cg)]
    project_in(0)
    for k in range(SCAN_SUBCHUNKS):
        if k + 1 < SCAN_SUBCHUNKS:
            project_in(k + 1)
        x = scan(k, x)
        if k > 0:
            project_out(k - 1)
    project_out(SCAN_SUBCHUNKS - 1)
    for g in range(n_cg):
        carry_scr[s, g] = x[g]


def _ssm(u, bb, cc, a_re, a_im, d_skip, t_chunk=512):
    s, r, width = u.shape
    assert r == SUBLANES // 2
    n_slab = width // SLAB
    hs = HALF_STATES
    return pl.pallas_call(
        functools.partial(_ssm_kernel, t_chunk=t_chunk),
        grid=(s // t_chunk, n_slab),
        in_specs=[pl.BlockSpec((t_chunk, SUBLANES // 2, SLAB), lambda c, sl: (c, 0, sl)),
                  pl.BlockSpec((1, 2 * SLAB, 2 * hs), lambda c, sl: (sl, 0, 0)),
                  pl.BlockSpec((1, 2 * hs, 2 * SLAB), lambda c, sl: (sl, 0, 0)),
                  pl.BlockSpec((1, SUBLANES, hs), lambda c, sl: (sl, 0, 0)),
                  pl.BlockSpec((1, SUBLANES, hs), lambda c, sl: (sl, 0, 0)),
                  pl.BlockSpec((1, SLAB), lambda c, sl: (0, sl))],
        out_specs=pl.BlockSpec((t_chunk, SUBLANES // 2, SLAB), lambda c, sl: (c, 0, sl)),
        out_shape=jax.ShapeDtypeStruct((s, SUBLANES // 2, width), F32),
        scratch_shapes=[pltpu.VMEM((2 * hs // LANES, SUBLANES * t_chunk, LANES), F32),
                        pltpu.VMEM((n_slab, 2 * hs // LANES, SUBLANES, LANES), F32)],
        compiler_params=_params(("arbitrary", "arbitrary")),
        name="s5_scan",
    )(u, bb, cc, a_re, a_im, d_skip.reshape(1, width))


def _mix_kernel(a0_ref, a1_ref, y_ref, x_ref, mod_ref, wglu_ref, bglu_ref, ga_ref, gs_ref, wo_ref, lng_ref, lnb_ref,
                o_ref):
    aw = a0_ref.shape[1]
    a = _rms_norm_rows(jnp.concatenate([a0_ref[...], a1_ref[...]], axis=0), ga_ref[...])
    y = jax.nn.gelu(y_ref[...])
    z = y * jax.nn.sigmoid(jnp.dot(y.astype(BF16), wglu_ref[...], preferred_element_type=F32) + bglu_ref[...])
    sn = _rms_norm_rows(z, gs_ref[...])
    mix = (jnp.dot(a.astype(BF16), wo_ref[0:aw, :], preferred_element_type=F32)
           + jnp.dot(sn.astype(BF16), wo_ref[aw:, :], preferred_element_type=F32))
    gate = mod_ref[0, 5:6, :]
    o_ref[...] = _residual_layer_norm(x_ref[...], mix, 1.0 + gate, lng_ref[...], lnb_ref[...])


def _mix(attn, y_rows, x2d, mod, w_glu, b_glu, g_attn, g_ssm, w_out, ln_g, ln_b, *, seq):
    n, d = x2d.shape
    _, _, n_steps, blk, aw = attn.shape
    tm = 2 * blk
    sw = y_rows.shape[1] // (n // seq)
    tiles_per_batch = seq // tm
    const = lambda m: (0, 0)

    def attn_spec(e):
        def index(m):
            i = 2 * (m % tiles_per_batch) + e
            folded = i >= n_steps
            return (m // tiles_per_batch, folded.astype(jnp.int32), jnp.where(folded, 2 * n_steps - 1 - i, i), 0, 0)
        return pl.BlockSpec((None, None, None, blk, aw), index)

    return pl.pallas_call(
        _mix_kernel,
        grid=(n // tm,),
        in_specs=[attn_spec(0), attn_spec(1),
                  pl.BlockSpec((tm, sw), lambda m: (m % tiles_per_batch, m // tiles_per_batch)),
                  pl.BlockSpec((tm, d), lambda m: (m, 0)),
                  pl.BlockSpec((1, N_MOD, d), lambda m: (m // tiles_per_batch, 0, 0)),
                  pl.BlockSpec((sw, sw), const),
                  pl.BlockSpec((1, sw), const),
                  pl.BlockSpec((1, aw), const),
                  pl.BlockSpec((1, sw), const),
                  pl.BlockSpec((aw + sw, d), const),
                  pl.BlockSpec((1, d), const),
                  pl.BlockSpec((1, d), const)],
        out_specs=pl.BlockSpec((tm, d), lambda m: (m, 0)),
        out_shape=jax.ShapeDtypeStruct((n, d), F32),
        compiler_params=_params(("parallel",)),
        name="mix",
    )(attn, attn, y_rows, x2d, mod, w_glu, b_glu.reshape(1, sw), g_attn.reshape(1, aw), g_ssm.reshape(1, sw),
      w_out, ln_g.reshape(1, d), ln_b.reshape(1, d))


def kernel(x, c, w_ada, b_ada, ffn1_w_gate, ffn1_w_up, ffn1_w_down, ln1_g, ln1_b, w_in, attn_norm_g, ssm_lambda_re, ssm_lambda_im, ssm_log_dt, ssm_b_re, ssm_b_im, ssm_c_re, ssm_c_im, ssm_d, ssm_w_glu, ssm_b_glu, ssm_norm_g, w_out, ln2_g, ln2_b, ffn2_w_gate, ffn2_w_up, ffn2_w_down, ln3_g, ln3_b):
    b, s, d = x.shape
    assert w_ada.shape[0] == DEPTH
    slopes = 2.0 ** (-(8.0 / N_HEADS) * jnp.arange(1, N_HEADS + 1, dtype=F32))
    c_pad = jnp.pad(c, ((0, SUBLANES - b), (0, 0)))
    x2d = x.reshape(b * s, d)
    for l in range(DEPTH):
        mod = _adaln(c_pad, w_ada[l], b_ada[l])[:b].reshape(b, N_MOD, d)

        x2d, (w_in_bf, w_out_bf, w_glu_bf) = _ffn(
            x2d, mod, ffn1_w_gate[l].astype(BF16), ffn1_w_up[l].astype(BF16), ffn1_w_down[l].astype(BF16),
            ln1_g[l], ln1_b[l], mod_base=0, seq=s, cast_jobs=(w_in[l], w_out[l], ssm_w_glu[l]))

        q, k, v, u = _inproj(x2d.reshape(b, s, d), mod, w_in_bf)
        attn, (w2_gate, w2_up, w2_down) = _attention(
            q, k, v, slopes, cast_jobs=(ffn2_w_gate[l], ffn2_w_up[l], ffn2_w_down[l]))

        bb, cc, a_re, a_im = _ssm_tables(ssm_lambda_re[l], ssm_lambda_im[l], ssm_log_dt[l],
                                         ssm_b_re[l], ssm_b_im[l], ssm_c_re[l], ssm_c_im[l])
        sw = u.shape[2]
        y = _ssm(u, bb, cc, a_re, a_im, ssm_d[l])

        x2d = _mix(attn, y.reshape(s, b * sw), x2d, mod, w_glu_bf, ssm_b_glu[l],
                   attn_norm_g[l], ssm_norm_g[l], w_out_bf, ln2_g[l], ln2_b[l], seq=s)

        x2d, _ = _ffn(x2d, mod, w2_gate, w2_up, w2_down, ln3_g[l], ln3_b[l], mod_base=6, seq=s)
    return x2d.reshape(b, s, d)
```

```python
import functools
import math

import jax
import jax.numpy as jnp
from jax import lax
from jax.experimental import pallas as pl
from jax.experimental.pallas import tpu as pltpu

F32 = jnp.float32
BF16 = jnp.bfloat16

N_HEADS = 8
HEAD_DIM = 128
MOBA_BLOCK = 256
MOBA_TOPK = 3
SSM_GROUP = 16
SSM_STATE = 64
N_MOD = 9
LN_EPS = 1e-5
NEG_INF = -1e30
LOG2E = math.log2(math.e)
DEPTH = 1
ALPHA = (2.0 * DEPTH) ** 0.25
Q_PRESCALE = HEAD_DIM ** -0.5 * LOG2E

LANES = 128
SUBLANES = 8
VMEM_LIMIT_BYTES = 56 * 1024 * 1024

SLAB = LANES
HALF_STATES = 256
SCAN_SUBCHUNKS = 8
FFN_TILE = 512


def _params(sem):
    return pltpu.CompilerParams(dimension_semantics=sem, vmem_limit_bytes=VMEM_LIMIT_BYTES)


def _residual_layer_norm(x, y, coef, g, b):
    r = x + (coef * (1.0 / ALPHA)) * y
    mu = jnp.mean(r, axis=-1, keepdims=True)
    d = r - mu
    var = jnp.mean(d * d, axis=-1, keepdims=True)
    return d * lax.rsqrt(var + LN_EPS / (ALPHA * ALPHA)) * g + b


def _rms_norm_rows(y, g):
    return y * lax.rsqrt(jnp.mean(y * y, axis=-1, keepdims=True) + LN_EPS) * g


def _adaln_kernel(c_ref, w_ref, b_ref, o_ref):
    c = c_ref[...]
    c_act = (c * jax.nn.sigmoid(c)).astype(BF16)
    o_ref[...] = jnp.dot(c_act, w_ref[...].astype(BF16), preferred_element_type=F32) + b_ref[...]


def _adaln(c_pad, w_ada, b_ada, tn=1024):
    rows, d = c_pad.shape
    n = w_ada.shape[1]
    return pl.pallas_call(
        _adaln_kernel,
        grid=(n // tn,),
        in_specs=[pl.BlockSpec((rows, d), lambda j: (0, 0)),
                  pl.BlockSpec((d, tn), lambda j: (0, j)),
                  pl.BlockSpec((1, tn), lambda j: (0, j))],
        out_specs=pl.BlockSpec((rows, tn), lambda j: (0, j)),
        out_shape=jax.ShapeDtypeStruct((rows, n), F32),
        compiler_params=_params(("arbitrary",)),
        name="adaln",
    )(c_pad, w_ada, b_ada.reshape(1, n))


def _cast_side_job(cast_in, cast_out):
    for src, dst in zip(cast_in, cast_out):
        dst[...] = src[...].astype(BF16)


def _ffn_kernel(*refs, mod_base, n_cast, odd_tiles):
    x_ref, mod_ref, wga_ref, wua_ref, wda_ref, wgb_ref, wub_ref, wdb_ref, lng_ref, lnb_ref = refs[:10]
    cast_in = refs[10:10 + n_cast]
    o_ref = refs[10 + n_cast]
    cast_out = refs[11 + n_cast:11 + 2 * n_cast]
    u_scr, acc_scr = refs[11 + 2 * n_cast:]
    f = pl.program_id(1)
    last = pl.num_programs(1) - 1

    @pl.when(f == 0)
    def _():
        shift = mod_ref[0, mod_base:mod_base + 1, :]
        scale = mod_ref[0, mod_base + 1:mod_base + 2, :]
        u_scr[...] = (x_ref[...] * (1.0 + scale) + shift).astype(BF16)
        acc_scr[...] = jnp.zeros_like(acc_scr)

    def swiglu(wg_ref, wu_ref, wd_ref):
        u = u_scr[...]
        g = jnp.dot(u, wg_ref[...], preferred_element_type=F32)
        up = jnp.dot(u, wu_ref[...], preferred_element_type=F32)
        h = (g * jax.nn.sigmoid(g)) * up
        return jnp.dot(h.astype(BF16), wd_ref[...], preferred_element_type=F32)

    def both_tiles():
        y = swiglu(wga_ref, wua_ref, wda_ref)
        _cast_side_job(cast_in, cast_out)
        return y + swiglu(wgb_ref, wub_ref, wdb_ref)

    def finish(y):
        gate = mod_ref[0, mod_base + 2:mod_base + 3, :]
        o_ref[...] = _residual_layer_norm(x_ref[...], acc_scr[...] + y, 0.5 * (1.0 + gate), lng_ref[...], lnb_ref[...])

    @pl.when(f < last)
    def _():
        acc_scr[...] += both_tiles()

    @pl.when(f == last)
    def _():
        if odd_tiles:
            _cast_side_job(cast_in, cast_out)
            finish(swiglu(wga_ref, wua_ref, wda_ref))
        else:
            finish(both_tiles())


def _cast_job_spec(w, grid):
    rows, cols = w.shape
    bf16_rows = 2 * SUBLANES
    steps = math.prod(grid)
    n_parts = max(k for k in range(1, steps + 1)
                  if steps % k == 0 and rows % k == 0 and (rows // k) % bf16_rows == 0)
    visits = steps // n_parts

    def index(*ids):
        step = 0
        for i, extent in zip(ids, grid):
            step = step * extent + i
        return step // visits, 0

    return pl.BlockSpec((rows // n_parts, cols), index)


def _ffn(x2d, mod, wg, wu, wd, ln_g, ln_b, *, mod_base, seq, cast_jobs=(), tm=512):
    n, d = x2d.shape
    tf = FFN_TILE
    n_tiles = wg.shape[1] // tf
    tiles_per_batch = seq // tm
    grid = (n // tm, (n_tiles + 1) // 2)
    tile_a = lambda f: 2 * f
    tile_b = lambda f: jnp.minimum(2 * f + 1, n_tiles - 1 - n_tiles % 2)
    cast_specs = [_cast_job_spec(w, grid) for w in cast_jobs]
    out = pl.pallas_call(
        functools.partial(_ffn_kernel, mod_base=mod_base, n_cast=len(cast_jobs), odd_tiles=n_tiles % 2 == 1),
        grid=grid,
        in_specs=[pl.BlockSpec((tm, d), lambda m, f: (m, 0)),
                  pl.BlockSpec((1, N_MOD, d), lambda m, f: (m // tiles_per_batch, 0, 0)),
                  pl.BlockSpec((d, tf), lambda m, f: (0, tile_a(f))),
                  pl.BlockSpec((d, tf), lambda m, f: (0, tile_a(f))),
                  pl.BlockSpec((tf, d), lambda m, f: (tile_a(f), 0)),
                  pl.BlockSpec((d, tf), lambda m, f: (0, tile_b(f))),
                  pl.BlockSpec((d, tf), lambda m, f: (0, tile_b(f))),
                  pl.BlockSpec((tf, d), lambda m, f: (tile_b(f), 0)),
                  pl.BlockSpec((1, d), lambda m, f: (0, 0)),
                  pl.BlockSpec((1, d), lambda m, f: (0, 0))] + cast_specs,
        out_specs=[pl.BlockSpec((tm, d), lambda m, f: (m, 0))] + cast_specs,
        out_shape=[jax.ShapeDtypeStruct((n, d), F32)] + [jax.ShapeDtypeStruct(w.shape, BF16) for w in cast_jobs],
        scratch_shapes=[pltpu.VMEM((tm, d), BF16), pltpu.VMEM((tm, d), F32)],
        compiler_params=_params(("arbitrary", "arbitrary")),
        name="ffn",
    )(x2d, mod, wg, wu, wd, wg, wu, wd, ln_g.reshape(1, d), ln_b.reshape(1, d), *cast_jobs)
    return out[0], out[1:]


def _inproj_kernel(x_ref, mod_ref, w_ref, q_ref, k_ref, v_ref, u_ref):
    n_batch, tt, width = q_ref.shape
    xm = jnp.concatenate(
        [(x_ref[b] * (1.0 + mod_ref[b, 4:5, :]) + mod_ref[b, 3:4, :]).astype(BF16) for b in range(n_batch)], axis=0)

    def project(col):
        return jnp.dot(xm, w_ref[:, col * width:(col + 1) * width], preferred_element_type=F32)

    q_ref[...] = (project(0) * Q_PRESCALE).astype(BF16).reshape(n_batch, tt, width)
    k_ref[...] = project(1).astype(BF16).reshape(n_batch, tt, width)
    v_ref[...] = project(2).astype(BF16).reshape(n_batch, tt, width)
    u = project(3)
    for b in range(n_batch):
        u_ref[:, b, :] = u[b * tt:(b + 1) * tt, :]


def _inproj(x, mod, w_in, tt=128):
    n_batch, seq, d = x.shape
    width = w_in.shape[1] // 4
    qkv_spec = pl.BlockSpec((n_batch, tt, width), lambda m: (0, m, 0))
    return pl.pallas_call(
        _inproj_kernel,
        grid=(seq // tt,),
        in_specs=[pl.BlockSpec((n_batch, tt, d), lambda m: (0, m, 0)),
                  pl.BlockSpec((n_batch, N_MOD, d), lambda m: (0, 0, 0)),
                  pl.BlockSpec((d, 4 * width), lambda m: (0, 0), pipeline_mode=pl.Buffered(1))],
        out_specs=[qkv_spec, qkv_spec, qkv_spec, pl.BlockSpec((tt, n_batch, width), lambda m: (m, 0, 0))],
        out_shape=[jax.ShapeDtypeStruct((n_batch, seq, width), BF16)] * 3
                  + [jax.ShapeDtypeStruct((seq, n_batch, width), F32)],
        compiler_params=_params(("arbitrary",)),
        name="inproj",
    )(x, mod, w_in)


def _dot_nt(a, b):
    return lax.dot_general(a, b, (((1,), (1,)), ((), ())), preferred_element_type=F32)


def _attn_kernel(*refs, n_blocks, hg, n_cast):
    slopes_ref, qa_ref, qb_ref, k_ref, v_ref = refs[:5]
    cast_in = refs[5:5 + n_cast]
    o_ref = refs[5 + n_cast]
    cast_out = refs[6 + n_cast:6 + 2 * n_cast]
    (kmean_scr, vt_scr, base_scr, q_scr, pk_scr, m_scr, acc_scr,
     sa_scr, sb_scr, pa_scr, pb_scr) = refs[6 + 2 * n_cast:]
    g = pl.program_id(1)
    p = pl.program_id(2)
    blk = MOBA_BLOCK
    dh = HEAD_DIM
    slope = [slopes_ref[g * hg + h] for h in range(hg)]
    hcols = [slice(h * dh, (h + 1) * dh) for h in range(hg)]
    q_refs = (qa_ref, qb_ref)
    tile_blk = (p, n_blocks - 1 - p)

    key_id = lax.broadcasted_iota(jnp.int32, (blk, blk), 0)
    qry_id = lax.broadcasted_iota(jnp.int32, (blk, blk), 1)
    rel = (qry_id - key_id).astype(F32)

    @pl.when(p == 0)
    def _():
        row_id = lax.broadcasted_iota(jnp.int32, (vt_scr.shape[1] - dh, vt_scr.shape[2]), 0)
        ones_row = jnp.where(row_id == 0, 1.0, 0.0).astype(BF16)
        for h in range(hg):
            base_scr[h] = (-LOG2E * slope[h]) * rel
            vt_scr[h, dh:, :] = ones_row
            for j in range(n_blocks):
                rows = slice(j * blk, (j + 1) * blk)
                kmean_scr[h, j:j + 1, :] = jnp.mean(k_ref[0, rows, hcols[h]].astype(F32), axis=0, keepdims=True)
                vt_scr[h, 0:dh, rows] = v_ref[0, rows, hcols[h]].astype(F32).T.astype(BF16)

    def item(n):
        tile = (n >= p).astype(jnp.int32)
        return tile, n - p * tile, jnp.where(tile == 0, tile_blk[0], tile_blk[1])

    def scores_into(n, s_scr):
        tile, j, _ = item(n)
        keys = pl.ds(pl.multiple_of(j * blk, blk), blk)
        for h in range(hg):
            s_scr[h] = _dot_nt(k_ref[0, keys, hcols[h]], q_scr[tile, h])

    def softmax_stage(n, s_scr, p_scr):
        tile, j, i_blk = item(n)
        j_f = j.astype(F32)
        i_f = i_blk.astype(F32)
        alphas = []
        for h in range(hg):
            selected = ((pk_scr[tile, h, 0:1, :] == j_f) | (pk_scr[tile, h, 1:2, :] == j_f)
                        | (pk_scr[tile, h, 2:3, :] == j_f))
            row_bias = jnp.where(selected, (j_f - i_f) * (float(blk) * LOG2E * slope[h]), NEG_INF)
            x = s_scr[h] + base_scr[h]
            m_old = m_scr[tile, h]
            m_new = jnp.maximum(m_old, jnp.max(x, axis=0, keepdims=True) + row_bias)
            p_scr[h] = jnp.exp2(x - (m_new - row_bias)).astype(BF16)
            m_scr[tile, h] = m_new
            alphas.append(jnp.exp2(m_old - m_new))
        return tuple(alphas)

    def accumulate(n, p_scr, alphas):
        tile, j, _ = item(n)
        keys = pl.ds(pl.multiple_of(j * blk, blk), blk)
        pv = [jnp.dot(vt_scr[h, :, keys], p_scr[h], preferred_element_type=F32) for h in range(hg)]
        for h in range(hg):
            acc_scr[tile, h] = alphas[h] * acc_scr[tile, h] + pv[h]

    for t in range(2):
        for h in range(hg):
            q_scr[t, h] = q_refs[t][0, 0, :, hcols[h]]
    scores_into(0, sa_scr)
    own = [pl.ds(pl.multiple_of(tile_blk[t] * blk, blk), blk) for t in range(2)]
    gates = [[_dot_nt(kmean_scr[h].astype(BF16), q_scr[t, h]) for h in range(hg)] for t in range(2)]
    s_own = [[_dot_nt(k_ref[0, own[t], hcols[h]], q_scr[t, h]) for h in range(hg)] for t in range(2)]
    _cast_side_job(cast_in, cast_out)
    p_own = [[None] * hg for _ in range(2)]
    for t in range(2):
        i_f = tile_blk[t].astype(F32)
        for h in range(hg):
            blk_id = lax.broadcasted_iota(jnp.int32, gates[t][h].shape, 0).astype(F32)
            work = jnp.where(blk_id < i_f, gates[t][h], NEG_INF)
            for r in range(MOBA_TOPK):
                mx = jnp.max(work, axis=0, keepdims=True)
                idx = jnp.min(jnp.where(work == mx, blk_id, float(n_blocks)), axis=0, keepdims=True)
                work = jnp.where(blk_id == idx, -jnp.inf, work)
                pk_scr[t, h, r:r + 1, :] = jnp.where(idx < i_f, idx, -1.0)
            s = jnp.where(rel >= 0.0, s_own[t][h] + base_scr[h], NEG_INF)
            m = jnp.max(s, axis=0, keepdims=True)
            m_scr[t, h] = m
            p_own[t][h] = jnp.exp2(s - m).astype(BF16)
    for t in range(2):
        pv_own = [jnp.dot(vt_scr[h, :, own[t]], p_own[t][h], preferred_element_type=F32) for h in range(hg)]
        for h in range(hg):
            acc_scr[t, h] = pv_own[h]

    s_slots = (sa_scr, sb_scr)
    p_slots = (pa_scr, pb_scr)
    a_prev = None
    for n in range(n_blocks - 1):
        if n + 1 < n_blocks - 1:
            scores_into(n + 1, s_slots[(n + 1) % 2])
        a_cur = softmax_stage(n, s_slots[n % 2], p_slots[n % 2])
        if n > 0:
            accumulate(n - 1, p_slots[(n - 1) % 2], a_prev)
        a_prev = a_cur
    accumulate(n_blocks - 2, p_slots[(n_blocks - 2) % 2], a_prev)
    for t in range(2):
        for h in range(hg):
            acc = acc_scr[t, h]
            o_ref[0, t, 0, :, hcols[h]] = (acc[0:dh] / acc[dh:dh + 1]).T


def _attention(q, k, v, slopes, cast_jobs=(), hg=4):
    b, s, width = q.shape
    n_heads = width // HEAD_DIM
    n_blocks = s // MOBA_BLOCK
    assert n_blocks % 2 == 0 and n_heads % hg == 0
    n_steps = n_blocks // 2
    gw = hg * HEAD_DIM
    ones_rows = 16
    kv_spec = pl.BlockSpec((1, s, gw), lambda bi, g, p: (bi, 0, g))
    slot = lambda dims, dt: pltpu.VMEM(dims, dt)
    grid = (b, n_heads // hg, n_steps)
    cast_specs = [_cast_job_spec(w, grid) for w in cast_jobs]
    out = pl.pallas_call(
        functools.partial(_attn_kernel, n_blocks=n_blocks, hg=hg, n_cast=len(cast_jobs)),
        grid=grid,
        in_specs=[pl.BlockSpec(memory_space=pltpu.SMEM),
                  pl.BlockSpec((1, 1, MOBA_BLOCK, gw), lambda bi, g, p: (bi, p, 0, g)),
                  pl.BlockSpec((1, 1, MOBA_BLOCK, gw), lambda bi, g, p: (bi, n_blocks - 1 - p, 0, g)),
                  kv_spec, kv_spec] + cast_specs,
        out_specs=[pl.BlockSpec((1, 2, 1, MOBA_BLOCK, gw), lambda bi, g, p: (bi, 0, p, 0, g))] + cast_specs,
        out_shape=([jax.ShapeDtypeStruct((b, 2, n_steps, MOBA_BLOCK, width), F32)]
                   + [jax.ShapeDtypeStruct(w.shape, BF16) for w in cast_jobs]),
        scratch_shapes=[slot((hg, n_blocks, HEAD_DIM), F32),
                        slot((hg, HEAD_DIM + ones_rows, s), BF16),
                        slot((hg, MOBA_BLOCK, MOBA_BLOCK), F32),
                        slot((2, hg, MOBA_BLOCK, HEAD_DIM), BF16),
                        slot((2, hg, SUBLANES, MOBA_BLOCK), F32),
                        slot((2, hg, 1, MOBA_BLOCK), F32),
                        slot((2, hg, HEAD_DIM + ones_rows, MOBA_BLOCK), F32),
                        slot((hg, MOBA_BLOCK, MOBA_BLOCK), F32), slot((hg, MOBA_BLOCK, MOBA_BLOCK), F32),
                        slot((hg, MOBA_BLOCK, MOBA_BLOCK), BF16), slot((hg, MOBA_BLOCK, MOBA_BLOCK), BF16)],
        compiler_params=_params(("arbitrary", "arbitrary", "arbitrary")),
        name="moba_attn",
    )(slopes, q.reshape(b, n_blocks, MOBA_BLOCK, width), q.reshape(b, n_blocks, MOBA_BLOCK, width), k, v, *cast_jobs)
    return out[0], out[1:]


def _ssm_tables(lam_re, lam_im, log_dt, b_re, b_im, c_re, c_im):
    g_total, p = lam_re.shape
    n_slab = g_total * SSM_GROUP // SLAB
    gps = SLAB // SSM_GROUP
    gph = gps // 2
    dt = jnp.exp(log_dt)[:, None]
    decay = jnp.exp(lam_re * dt)
    ab_re = decay * jnp.cos(lam_im * dt)
    ab_im = decay * jnp.sin(lam_im * dt)
    den = lam_re * lam_re + lam_im * lam_im
    coef_re = ((ab_re - 1.0) * lam_re + ab_im * lam_im) / den
    coef_im = (ab_im * lam_re - (ab_re - 1.0) * lam_im) / den
    bb_re = coef_re[..., None] * b_re - coef_im[..., None] * b_im
    bb_im = coef_re[..., None] * b_im + coef_im[..., None] * b_re
    eye = jnp.eye(gps, dtype=F32)

    bbri = jnp.stack([bb_re, bb_im], axis=0).reshape(2, n_slab, gps, p, SSM_GROUP)
    bb = jnp.einsum('rsgph,gk->sghkrp', bbri, eye)
    bb = bb.reshape(n_slab, gps, SSM_GROUP, 2, gph, 2, p).transpose(0, 1, 2, 3, 5, 4, 6)
    bb = bb.reshape(n_slab, SLAB, 2, 2 * HALF_STATES).transpose(0, 2, 1, 3).reshape(n_slab, 2 * SLAB, 2 * HALF_STATES)

    ccri = jnp.stack([c_re, -c_im], axis=0).reshape(2, n_slab, gps, SSM_GROUP, p)
    cc = jnp.einsum('rsghp,gk->skrpgh', ccri, eye)
    cc = cc.reshape(n_slab, 2, gph, 2, p, gps, SSM_GROUP).transpose(0, 1, 3, 2, 4, 5, 6)
    cc = cc.reshape(n_slab, 2, 2 * HALF_STATES, SLAB).transpose(0, 2, 1, 3).reshape(n_slab, 2 * HALF_STATES, 2 * SLAB)

    def rows(a):
        a = a.reshape(n_slab, 2, 1, HALF_STATES)
        return jnp.broadcast_to(a, (n_slab, 2, 4, HALF_STATES)).reshape(n_slab, SUBLANES, HALF_STATES)

    return bb.astype(BF16), cc.astype(BF16), rows(ab_re), rows(ab_im)


def _ssm_kernel(u_ref, bb_ref, cc_ref, are_ref, aim_ref, d_ref, y_ref, x_scr, carry_scr, *, t_chunk):
    c = pl.program_id(0)
    s = pl.program_id(1)
    hs = HALF_STATES
    n_cg = 2 * hs // LANES
    n_re = hs // LANES
    half_rows = SUBLANES // 2


    @pl.when(c == 0)
    def _():
        carry_scr[s] = jnp.zeros((n_cg, SUBLANES, LANES), F32)

    a_re = [are_ref[0, :, g * LANES:(g + 1) * LANES] for g in range(n_re)]
    a_im = [aim_ref[0, :, g * LANES:(g + 1) * LANES] for g in range(n_re)]
    sub = t_chunk // SCAN_SUBCHUNKS

    def project_in(k):
        part = slice(k * sub * SUBLANES, (k + 1) * sub * SUBLANES)
        u4 = u_ref[k * sub:(k + 1) * sub]
        u3 = jnp.concatenate([u4, jnp.zeros_like(u4)], axis=1)
        u_lo = u3.reshape(sub * SUBLANES, SLAB)
        u_hi = pltpu.roll(u3, half_rows, 1).reshape(sub * SUBLANES, SLAB)
        lhs = jnp.concatenate([u_lo, u_hi], axis=1).astype(BF16)
        bu = jnp.dot(lhs, bb_ref[0], preferred_element_type=F32)
        for g in range(n_cg):
            x_scr[g, part, :] = bu[:, g * LANES:(g + 1) * LANES]

    def scan(k, x):
        for t in range(k * sub, (k + 1) * sub):
            idx = slice(t * SUBLANES, (t + 1) * SUBLANES)
            new = [None] * n_cg
            for g in range(n_re):
                x_re, x_im = x[g], x[n_re + g]
                new[g] = a_re[g] * x_re - a_im[g] * x_im + x_scr[g, idx, :]
                new[n_re + g] = a_re[g] * x_im + a_im[g] * x_re + x_scr[n_re + g, idx, :]
            for g in range(n_cg):
                x_scr[g, idx, :] = new[g]
            x = new
        return x

    def project_out(k):
        part = slice(k * sub * SUBLANES, (k + 1) * sub * SUBLANES)
        steps = slice(k * sub, (k + 1) * sub)
        xs = jnp.concatenate([x_scr[g, part, :] for g in range(n_cg)], axis=1).astype(BF16)
        yy = jnp.dot(xs, cc_ref[0], preferred_element_type=F32)
        y_half1 = pltpu.roll(yy[:, SLAB:].reshape(sub, SUBLANES, SLAB), half_rows, 1)
        y3 = yy[:, :SLAB].reshape(sub, SUBLANES, SLAB) + y_half1
        y_ref[steps] = y3[:, 0:half_rows, :] + d_ref[...] * u_ref[steps]

    x = [carry_scr[s, g] for g in range(n_cg)]
    project_in(0)
    for k in range(SCAN_SUBCHUNKS):
        if k + 1 < SCAN_SUBCHUNKS:
            project_in(k + 1)
        x = scan(k, x)
        if k > 0:
            project_out(k - 1)
    project_out(SCAN_SUBCHUNKS - 1)
    for g in range(n_cg):
        carry_scr[s, g] = x[g]


def _ssm(u, bb, cc, a_re, a_im, d_skip, t_chunk=512):
    s, r, width = u.shape
    assert r == SUBLANES // 2
    n_slab = width // SLAB
    hs = HALF_STATES
    return pl.pallas_call(
        functools.partial(_ssm_kernel, t_chunk=t_chunk),
        grid=(s // t_chunk, n_slab),
        in_specs=[pl.BlockSpec((t_chunk, SUBLANES // 2, SLAB), lambda c, sl: (c, 0, sl)),
                  pl.BlockSpec((1, 2 * SLAB, 2 * hs), lambda c, sl: (sl, 0, 0)),
                  pl.BlockSpec((1, 2 * hs, 2 * SLAB), lambda c, sl: (sl, 0, 0)),
                  pl.BlockSpec((1, SUBLANES, hs), lambda c, sl: (sl, 0, 0)),
                  pl.BlockSpec((1, SUBLANES, hs), lambda c, sl: (sl, 0, 0)),
                  pl.BlockSpec((1, SLAB), lambda c, sl: (0, sl))],
        out_specs=pl.BlockSpec((t_chunk, SUBLANES // 2, SLAB), lambda c, sl: (c, 0, sl)),
        out_shape=jax.ShapeDtypeStruct((s, SUBLANES // 2, width), F32),
        scratch_shapes=[pltpu.VMEM((2 * hs // LANES, SUBLANES * t_chunk, LANES), F32),
                        pltpu.VMEM((n_slab, 2 * hs // LANES, SUBLANES, LANES), F32)],
        compiler_params=_params(("arbitrary", "arbitrary")),
        name="s5_scan",
    )(u, bb, cc, a_re, a_im, d_skip.reshape(1, width))


def _mix_kernel(a0_ref, a1_ref, y_ref, x_ref, mod_ref, wglu_ref, bglu_ref, ga_ref, gs_ref, wo_ref, lng_ref, lnb_ref,
                o_ref):
    aw = a0_ref.shape[1]
    a = _rms_norm_rows(jnp.concatenate([a0_ref[...], a1_ref[...]], axis=0), ga_ref[...])
    y = jax.nn.gelu(y_ref[...])
    z = y * jax.nn.sigmoid(jnp.dot(y.astype(BF16), wglu_ref[...], preferred_element_type=F32) + bglu_ref[...])
    sn = _rms_norm_rows(z, gs_ref[...])
    mix = (jnp.dot(a.astype(BF16), wo_ref[0:aw, :], preferred_element_type=F32)
           + jnp.dot(sn.astype(BF16), wo_ref[aw:, :], preferred_element_type=F32))
    gate = mod_ref[0, 5:6, :]
    o_ref[...] = _residual_layer_norm(x_ref[...], mix, 1.0 + gate, lng_ref[...], lnb_ref[...])


def _mix(attn, y_rows, x2d, mod, w_glu, b_glu, g_attn, g_ssm, w_out, ln_g, ln_b, *, seq):
    n, d = x2d.shape
    _, _, n_steps, blk, aw = attn.shape
    tm = 2 * blk
    sw = y_rows.shape[1] // (n // seq)
    tiles_per_batch = seq // tm
    const = lambda m: (0, 0)

    def attn_spec(e):
        def index(m):
            i = 2 * (m % tiles_per_batch) + e
            folded = i >= n_steps
            return (m // tiles_per_batch, folded.astype(jnp.int32), jnp.where(folded, 2 * n_steps - 1 - i, i), 0, 0)
        return pl.BlockSpec((None, None, None, blk, aw), index)

    return pl.pallas_call(
        _mix_kernel,
        grid=(n // tm,),
        in_specs=[attn_spec(0), attn_spec(1),
                  pl.BlockSpec((tm, sw), lambda m: (m % tiles_per_batch, m // tiles_per_batch)),
                  pl.BlockSpec((tm, d), lambda m: (m, 0)),
                  pl.BlockSpec((1, N_MOD, d), lambda m: (m // tiles_per_batch, 0, 0)),
                  pl.BlockSpec((sw, sw), const),
                  pl.BlockSpec((1, sw), const),
                  pl.BlockSpec((1, aw), const),
                  pl.BlockSpec((1, sw), const),
                  pl.BlockSpec((aw + sw, d), const),
                  pl.BlockSpec((1, d), const),
                  pl.BlockSpec((1, d), const)],
        out_specs=pl.BlockSpec((tm, d), lambda m: (m, 0)),
        out_shape=jax.ShapeDtypeStruct((n, d), F32),
        compiler_params=_params(("parallel",)),
        name="mix",
    )(attn, attn, y_rows, x2d, mod, w_glu, b_glu.reshape(1, sw), g_attn.reshape(1, aw), g_ssm.reshape(1, sw),
      w_out, ln_g.reshape(1, d), ln_b.reshape(1, d))


def kernel(x, c, w_ada, b_ada, ffn1_w_gate, ffn1_w_up, ffn1_w_down, ln1_g, ln1_b, w_in, attn_norm_g, ssm_lambda_re, ssm_lambda_im, ssm_log_dt, ssm_b_re, ssm_b_im, ssm_c_re, ssm_c_im, ssm_d, ssm_w_glu, ssm_b_glu, ssm_norm_g, w_out, ln2_g, ln2_b, ffn2_w_gate, ffn2_w_up, ffn2_w_down, ln3_g, ln3_b):
    b, s, d = x.shape
    assert w_ada.shape[0] == DEPTH
    slopes = 2.0 ** (-(8.0 / N_HEADS) * jnp.arange(1, N_HEADS + 1, dtype=F32))
    c_pad = jnp.pad(c, ((0, SUBLANES - b), (0, 0)))
    x2d = x.reshape(b * s, d)
    for l in range(DEPTH):
        mod = _adaln(c_pad, w_ada[l], b_ada[l])[:b].reshape(b, N_MOD, d)

        x2d, (w_in_bf, w_out_bf, w_glu_bf) = _ffn(
            x2d, mod, ffn1_w_gate[l].astype(BF16), ffn1_w_up[l].astype(BF16), ffn1_w_down[l].astype(BF16),
            ln1_g[l], ln1_b[l], mod_base=0, seq=s, cast_jobs=(w_in[l], w_out[l], ssm_w_glu[l]))

        q, k, v, u = _inproj(x2d.reshape(b, s, d), mod, w_in_bf)
        attn, (w2_gate, w2_up, w2_down) = _attention(
            q, k, v, slopes, cast_jobs=(ffn2_w_gate[l], ffn2_w_up[l], ffn2_w_down[l]))

        bb, cc, a_re, a_im = _ssm_tables(ssm_lambda_re[l], ssm_lambda_im[l], ssm_log_dt[l],
                                         ssm_b_re[l], ssm_b_im[l], ssm_c_re[l], ssm_c_im[l])
        sw = u.shape[2]
        y = _ssm(u, bb, cc, a_re, a_im, ssm_d[l])

        x2d = _mix(attn, y.reshape(s, b * sw), x2d, mod, w_glu_bf, ssm_b_glu[l],
                   attn_norm_g[l], ssm_norm_g[l], w_out_bf, ln2_g[l], ln2_b[l], seq=s)

        x2d, _ = _ffn(x2d, mod, w2_gate, w2_up, w2_down, ln3_g[l], ln3_b[l], mod_base=6, seq=s)
    return x2d.reshape(b, s, d)
```

```python
import functools
import math

import jax
import jax.numpy as jnp
from jax import lax
from jax.experimental import pallas as pl
from jax.experimental.pallas import tpu as pltpu

F32 = jnp.float32
BF16 = jnp.bfloat16

N_HEADS = 8
HEAD_DIM = 128
MOBA_BLOCK = 256
MOBA_TOPK = 3
SSM_GROUP = 16
SSM_STATE = 64
N_MOD = 9
LN_EPS = 1e-5
NEG_INF = -1e30
LOG2E = math.log2(math.e)
DEPTH = 1
ALPHA = (2.0 * DEPTH) ** 0.25
Q_PRESCALE = HEAD_DIM ** -0.5 * LOG2E

LANES = 128
SUBLANES = 8
VMEM_LIMIT_BYTES = 56 * 1024 * 1024

SLAB = LANES
HALF_STATES = 256
SCAN_SUBCHUNKS = 8
FFN_TILE = 512


def _params(sem):
    return pltpu.CompilerParams(dimension_semantics=sem, vmem_limit_bytes=VMEM_LIMIT_BYTES)


def _residual_layer_norm(x, y, coef, g, b):
    r = x + (coef * (1.0 / ALPHA)) * y
    mu = jnp.mean(r, axis=-1, keepdims=True)
    d = r - mu
    var = jnp.mean(d * d, axis=-1, keepdims=True)
    return d * lax.rsqrt(var + LN_EPS / (ALPHA * ALPHA)) * g + b


def _rms_norm_rows(y, g):
    return y * lax.rsqrt(jnp.mean(y * y, axis=-1, keepdims=True) + LN_EPS) * g


def _adaln_kernel(c_ref, w_ref, b_ref, o_ref):
    c = c_ref[...]
    c_act = (c * jax.nn.sigmoid(c)).astype(BF16)
    o_ref[...] = jnp.dot(c_act, w_ref[...].astype(BF16), preferred_element_type=F32) + b_ref[...]


def _adaln(c_pad, w_ada, b_ada, tn=1024):
    rows, d = c_pad.shape
    n = w_ada.shape[1]
    return pl.pallas_call(
        _adaln_kernel,
        grid=(n // tn,),
        in_specs=[pl.BlockSpec((rows, d), lambda j: (0, 0)),
                  pl.BlockSpec((d, tn), lambda j: (0, j)),
                  pl.BlockSpec((1, tn), lambda j: (0, j))],
        out_specs=pl.BlockSpec((rows, tn), lambda j: (0, j)),
        out_shape=jax.ShapeDtypeStruct((rows, n), F32),
        compiler_params=_params(("arbitrary",)),
        name="adaln",
    )(c_pad, w_ada, b_ada.reshape(1, n))


def _cast_side_job(cast_in, cast_out):
    for src, dst in zip(cast_in, cast_out):
        dst[...] = src[...].astype(BF16)


def _ffn_kernel(*refs, mod_base, n_cast, odd_tiles):
    x_ref, mod_ref, wga_ref, wua_ref, wda_ref, wgb_ref, wub_ref, wdb_ref, lng_ref, lnb_ref = refs[:10]
    cast_in = refs[10:10 + n_cast]
    o_ref = refs[10 + n_cast]
    cast_out = refs[11 + n_cast:11 + 2 * n_cast]
    u_scr, acc_scr = refs[11 + 2 * n_cast:]
    f = pl.program_id(1)
    last = pl.num_programs(1) - 1

    def swiglu(u, wg_ref, wu_ref, wd_ref):
        g = jnp.dot(u, wg_ref[...], preferred_element_type=F32)
        up = jnp.dot(u, wu_ref[...], preferred_element_type=F32)
        h = (g * jax.nn.sigmoid(g)) * up
        return jnp.dot(h.astype(BF16), wd_ref[...], preferred_element_type=F32)

    def both_tiles(u):
        y = swiglu(u, wga_ref, wua_ref, wda_ref)
        _cast_side_job(cast_in, cast_out)
        return y + swiglu(u, wgb_ref, wub_ref, wdb_ref)

    def finish(y):
        gate = mod_ref[0, mod_base + 2:mod_base + 3, :]
        o_ref[...] = _residual_layer_norm(x_ref[...], acc_scr[...] + y, 0.5 * (1.0 + gate), lng_ref[...], lnb_ref[...])

    @pl.when(f == 0)
    def _():
        shift = mod_ref[0, mod_base:mod_base + 1, :]
        scale = mod_ref[0, mod_base + 1:mod_base + 2, :]
        u = (x_ref[...] * (1.0 + scale) + shift).astype(BF16)
        u_scr[...] = u
        acc_scr[...] = both_tiles(u)

    @pl.when((f > 0) & (f < last))
    def _():
        acc_scr[...] += both_tiles(u_scr[...])

    @pl.when(f == last)
    def _():
        if odd_tiles:
            _cast_side_job(cast_in, cast_out)
            finish(swiglu(u_scr[...], wga_ref, wua_ref, wda_ref))
        else:
            finish(both_tiles(u_scr[...]))


def _cast_job_spec(w, grid):
    rows, cols = w.shape
    bf16_rows = 2 * SUBLANES
    steps = math.prod(grid)
    n_parts = max(k for k in range(1, steps + 1)
                  if steps % k == 0 and rows % k == 0 and (rows // k) % bf16_rows == 0)
    visits = steps // n_parts

    def index(*ids):
        step = 0
        for i, extent in zip(ids, grid):
            step = step * extent + i
        return step // visits, 0

    return pl.BlockSpec((rows // n_parts, cols), index)


def _ffn(x2d, mod, wg, wu, wd, ln_g, ln_b, *, mod_base, seq, cast_jobs=(), tm=512):
    n, d = x2d.shape
    tf = FFN_TILE
    n_tiles = wg.shape[1] // tf
    tiles_per_batch = seq // tm
    grid = (n // tm, (n_tiles + 1) // 2)
    assert grid[1] >= 2
    tile_a = lambda f: 2 * f
    tile_b = lambda f: jnp.minimum(2 * f + 1, n_tiles - 1 - n_tiles % 2)
    cast_specs = [_cast_job_spec(w, grid) for w in cast_jobs]
    out = pl.pallas_call(
        functools.partial(_ffn_kernel, mod_base=mod_base, n_cast=len(cast_jobs), odd_tiles=n_tiles % 2 == 1),
        grid=grid,
        in_specs=[pl.BlockSpec((tm, d), lambda m, f: (m, 0)),
                  pl.BlockSpec((1, N_MOD, d), lambda m, f: (m // tiles_per_batch, 0, 0)),
                  pl.BlockSpec((d, tf), lambda m, f: (0, tile_a(f))),
                  pl.BlockSpec((d, tf), lambda m, f: (0, tile_a(f))),
                  pl.BlockSpec((tf, d), lambda m, f: (tile_a(f), 0)),
                  pl.BlockSpec((d, tf), lambda m, f: (0, tile_b(f))),
                  pl.BlockSpec((d, tf), lambda m, f: (0, tile_b(f))),
                  pl.BlockSpec((tf, d), lambda m, f: (tile_b(f), 0)),
                  pl.BlockSpec((1, d), lambda m, f: (0, 0)),
                  pl.BlockSpec((1, d), lambda m, f: (0, 0))] + cast_specs,
        out_specs=[pl.BlockSpec((tm, d), lambda m, f: (m, 0))] + cast_specs,
        out_shape=[jax.ShapeDtypeStruct((n, d), F32)] + [jax.ShapeDtypeStruct(w.shape, BF16) for w in cast_jobs],
        scratch_shapes=[pltpu.VMEM((tm, d), BF16), pltpu.VMEM((tm, d), F32)],
        compiler_params=_params(("arbitrary", "arbitrary")),
        name="ffn",
    )(x2d, mod, wg, wu, wd, wg, wu, wd, ln_g.reshape(1, d), ln_b.reshape(1, d), *cast_jobs)
    return out[0], out[1:]


def _inproj_kernel(x_ref, mod_ref, w_ref, q_ref, k_ref, v_ref, u_ref):
    n_batch, tt, width = q_ref.shape
    xm = jnp.concatenate(
        [(x_ref[b] * (1.0 + mod_ref[b, 4:5, :]) + mod_ref[b, 3:4, :]).astype(BF16) for b in range(n_batch)], axis=0)

    def project(col):
        return jnp.dot(xm, w_ref[:, col * width:(col + 1) * width], preferred_element_type=F32)

    q_ref[...] = (project(0) * Q_PRESCALE).astype(BF16).reshape(n_batch, tt, width)
    k_ref[...] = project(1).astype(BF16).reshape(n_batch, tt, width)
    v_ref[...] = project(2).astype(BF16).reshape(n_batch, tt, width)
    u = project(3)
    for b in range(n_batch):
        u_ref[:, b, :] = u[b * tt:(b + 1) * tt, :]


def _inproj(x, mod, w_in, tt=128):
    n_batch, seq, d = x.shape
    width = w_in.shape[1] // 4
    qkv_spec = pl.BlockSpec((n_batch, tt, width), lambda m: (0, m, 0))
    return pl.pallas_call(
        _inproj_kernel,
        grid=(seq // tt,),
        in_specs=[pl.BlockSpec((n_batch, tt, d), lambda m: (0, m, 0)),
                  pl.BlockSpec((n_batch, N_MOD, d), lambda m: (0, 0, 0)),
                  pl.BlockSpec((d, 4 * width), lambda m: (0, 0), pipeline_mode=pl.Buffered(1))],
        out_specs=[qkv_spec, qkv_spec, qkv_spec, pl.BlockSpec((tt, n_batch, width), lambda m: (m, 0, 0))],
        out_shape=[jax.ShapeDtypeStruct((n_batch, seq, width), BF16)] * 3
                  + [jax.ShapeDtypeStruct((seq, n_batch, width), F32)],
        compiler_params=_params(("arbitrary",)),
        name="inproj",
    )(x, mod, w_in)


def _dot_nt(a, b):
    return lax.dot_general(a, b, (((1,), (1,)), ((), ())), preferred_element_type=F32)


def _attn_kernel(*refs, n_blocks, hg, n_cast):
    slopes_ref, qa_ref, qb_ref, k_ref, v_ref = refs[:5]
    cast_in = refs[5:5 + n_cast]
    o_ref = refs[5 + n_cast]
    cast_out = refs[6 + n_cast:6 + 2 * n_cast]
    (kmean_scr, vt_scr, base_scr, q_scr, pk_scr, m_scr, acc_scr,
     sa_scr, sb_scr, pa_scr, pb_scr) = refs[6 + 2 * n_cast:]
    g = pl.program_id(1)
    p = pl.program_id(2)
    blk = MOBA_BLOCK
    dh = HEAD_DIM
    slope = [slopes_ref[g * hg + h] for h in range(hg)]
    hcols = [slice(h * dh, (h + 1) * dh) for h in range(hg)]
    q_refs = (qa_ref, qb_ref)
    tile_blk = (p, n_blocks - 1 - p)

    key_id = lax.broadcasted_iota(jnp.int32, (blk, blk), 0)
    qry_id = lax.broadcasted_iota(jnp.int32, (blk, blk), 1)
    rel = (qry_id - key_id).astype(F32)

    @pl.when(p == 0)
    def _():
        row_id = lax.broadcasted_iota(jnp.int32, (vt_scr.shape[1] - dh, vt_scr.shape[2]), 0)
        ones_row = jnp.where(row_id == 0, 1.0, 0.0).astype(BF16)
        for h in range(hg):
            base_scr[h] = (-LOG2E * slope[h]) * rel
            vt_scr[h, dh:, :] = ones_row
            for j in range(n_blocks):
                rows = slice(j * blk, (j + 1) * blk)
                kmean_scr[h, j:j + 1, :] = jnp.mean(k_ref[0, rows, hcols[h]].astype(F32), axis=0, keepdims=True)
                vt_scr[h, 0:dh, rows] = v_ref[0, rows, hcols[h]].astype(F32).T.astype(BF16)

    def item(n):
        tile = (n >= p).astype(jnp.int32)
        return tile, n - p * tile, jnp.where(tile == 0, tile_blk[0], tile_blk[1])

    def scores_into(n, s_scr):
        tile, j, _ = item(n)
        keys = pl.ds(pl.multiple_of(j * blk, blk), blk)
        for h in range(hg):
            s_scr[h] = _dot_nt(k_ref[0, keys, hcols[h]], q_scr[tile, h])

    def softmax_stage(n, s_scr, p_scr):
        tile, j, i_blk = item(n)
        j_f = j.astype(F32)
        i_f = i_blk.astype(F32)
        alphas = []
        for h in range(hg):
            selected = ((pk_scr[tile, h, 0:1, :] == j_f) | (pk_scr[tile, h, 1:2, :] == j_f)
                        | (pk_scr[tile, h, 2:3, :] == j_f))
            row_bias = jnp.where(selected, (j_f - i_f) * (float(blk) * LOG2E * slope[h]), NEG_INF)
            x = s_scr[h] + base_scr[h]
            m_old = m_scr[tile, h]
            m_new = jnp.maximum(m_old, jnp.max(x, axis=0, keepdims=True) + row_bias)
            p_scr[h] = jnp.exp2(x - (m_new - row_bias)).astype(BF16)
            m_scr[tile, h] = m_new
            alphas.append(jnp.exp2(m_old - m_new))
        return tuple(alphas)

    def accumulate(n, p_scr, alphas):
        tile, j, _ = item(n)
        keys = pl.ds(pl.multiple_of(j * blk, blk), blk)
        pv = [jnp.dot(vt_scr[h, :, keys], p_scr[h], preferred_element_type=F32) for h in range(hg)]
        for h in range(hg):
            acc_scr[tile, h] = alphas[h] * acc_scr[tile, h] + pv[h]

    for t in range(2):
        for h in range(hg):
            q_scr[t, h] = q_refs[t][0, 0, :, hcols[h]]
    scores_into(0, sa_scr)
    own = [pl.ds(pl.multiple_of(tile_blk[t] * blk, blk), blk) for t in range(2)]
    gates = [[_dot_nt(kmean_scr[h].astype(BF16), q_scr[t, h]) for h in range(hg)] for t in range(2)]
    s_own = [[_dot_nt(k_ref[0, own[t], hcols[h]], q_scr[t, h]) for h in range(hg)] for t in range(2)]
    _cast_side_job(cast_in, cast_out)
    p_own = [[None] * hg for _ in range(2)]
    for t in range(2):
        i_f = tile_blk[t].astype(F32)
        for h in range(hg):
            blk_id = lax.broadcasted_iota(jnp.int32, gates[t][h].shape, 0).astype(F32)
            work = jnp.where(blk_id < i_f, gates[t][h], NEG_INF)
            for r in range(MOBA_TOPK):
                mx = jnp.max(work, axis=0, keepdims=True)
                idx = jnp.min(jnp.where(work == mx, blk_id, float(n_blocks)), axis=0, keepdims=True)
                work = jnp.where(blk_id == idx, -jnp.inf, work)
                pk_scr[t, h, r:r + 1, :] = jnp.where(idx < i_f, idx, -1.0)
            s = jnp.where(rel >= 0.0, s_own[t][h] + base_scr[h], NEG_INF)
            m = jnp.max(s, axis=0, keepdims=True)
            m_scr[t, h] = m
            p_own[t][h] = jnp.exp2(s - m).astype(BF16)
    for t in range(2):
        pv_own = [jnp.dot(vt_scr[h, :, own[t]], p_own[t][h], preferred_element_type=F32) for h in range(hg)]
        for h in range(hg):
            acc_scr[t, h] = pv_own[h]

    s_slots = (sa_scr, sb_scr)
    p_slots = (pa_scr, pb_scr)
    a_prev = None
    for n in range(n_blocks - 1):
        if n + 1 < n_blocks - 1:
            scores_into(n + 1, s_slots[(n + 1) % 2])
        a_cur = softmax_stage(n, s_slots[n % 2], p_slots[n % 2])
        if n > 0:
            accumulate(n - 1, p_slots[(n - 1) % 2], a_prev)
        a_prev = a_cur
    accumulate(n_blocks - 2, p_slots[(n_blocks - 2) % 2], a_prev)
    for t in range(2):
        for h in range(hg):
            acc = acc_scr[t, h]
            o_ref[0, t, 0, :, hcols[h]] = (acc[0:dh] / acc[dh:dh + 1]).T


def _attention(q, k, v, slopes, cast_jobs=(), hg=4):
    b, s, width = q.shape
    n_heads = width // HEAD_DIM
    n_blocks = s // MOBA_BLOCK
    assert n_blocks % 2 == 0 and n_heads % hg == 0
    n_steps = n_blocks // 2
    gw = hg * HEAD_DIM
    ones_rows = 16
    kv_spec = pl.BlockSpec((1, s, gw), lambda bi, g, p: (bi, 0, g))
    slot = lambda dims, dt: pltpu.VMEM(dims, dt)
    grid = (b, n_heads // hg, n_steps)
    cast_specs = [_cast_job_spec(w, grid) for w in cast_jobs]
    out = pl.pallas_call(
        functools.partial(_attn_kernel, n_blocks=n_blocks, hg=hg, n_cast=len(cast_jobs)),
        grid=grid,
        in_specs=[pl.BlockSpec(memory_space=pltpu.SMEM),
                  pl.BlockSpec((1, 1, MOBA_BLOCK, gw), lambda bi, g, p: (bi, p, 0, g)),
                  pl.BlockSpec((1, 1, MOBA_BLOCK, gw), lambda bi, g, p: (bi, n_blocks - 1 - p, 0, g)),
                  kv_spec, kv_spec] + cast_specs,
        out_specs=[pl.BlockSpec((1, 2, 1, MOBA_BLOCK, gw), lambda bi, g, p: (bi, 0, p, 0, g))] + cast_specs,
        out_shape=([jax.ShapeDtypeStruct((b, 2, n_steps, MOBA_BLOCK, width), F32)]
                   + [jax.ShapeDtypeStruct(w.shape, BF16) for w in cast_jobs]),
        scratch_shapes=[slot((hg, n_blocks, HEAD_DIM), F32),
                        slot((hg, HEAD_DIM + ones_rows, s), BF16),
                        slot((hg, MOBA_BLOCK, MOBA_BLOCK), F32),
                        slot((2, hg, MOBA_BLOCK, HEAD_DIM), BF16),
                        slot((2, hg, SUBLANES, MOBA_BLOCK), F32),
                        slot((2, hg, 1, MOBA_BLOCK), F32),
                        slot((2, hg, HEAD_DIM + ones_rows, MOBA_BLOCK), F32),
                        slot((hg, MOBA_BLOCK, MOBA_BLOCK), F32), slot((hg, MOBA_BLOCK, MOBA_BLOCK), F32),
                        slot((hg, MOBA_BLOCK, MOBA_BLOCK), BF16), slot((hg, MOBA_BLOCK, MOBA_BLOCK), BF16)],
        compiler_params=_params(("arbitrary", "arbitrary", "arbitrary")),
        name="moba_attn",
    )(slopes, q.reshape(b, n_blocks, MOBA_BLOCK, width), q.reshape(b, n_blocks, MOBA_BLOCK, width), k, v, *cast_jobs)
    return out[0], out[1:]


def _ssm_tables(lam_re, lam_im, log_dt, b_re, b_im, c_re, c_im):
    g_total, p = lam_re.shape
    n_slab = g_total * SSM_GROUP // SLAB
    gps = SLAB // SSM_GROUP
    gph = gps // 2
    dt = jnp.exp(log_dt)[:, None]
    decay = jnp.exp(lam_re * dt)
    ab_re = decay * jnp.cos(lam_im * dt)
    ab_im = decay * jnp.sin(lam_im * dt)
    den = lam_re * lam_re + lam_im * lam_im
    coef_re = ((ab_re - 1.0) * lam_re + ab_im * lam_im) / den
    coef_im = (ab_im * lam_re - (ab_re - 1.0) * lam_im) / den
    bb_re = coef_re[..., None] * b_re - coef_im[..., None] * b_im
    bb_im = coef_re[..., None] * b_im + coef_im[..., None] * b_re
    eye = jnp.eye(gps, dtype=F32)

    bbri = jnp.stack([bb_re, bb_im], axis=0).reshape(2, n_slab, gps, p, SSM_GROUP)
    bb = jnp.einsum('rsgph,gk->sghkrp', bbri, eye)
    bb = bb.reshape(n_slab, gps, SSM_GROUP, 2, gph, 2, p).transpose(0, 1, 2, 3, 5, 4, 6)
    bb = bb.reshape(n_slab, SLAB, 2, 2 * HALF_STATES).transpose(0, 2, 1, 3).reshape(n_slab, 2 * SLAB, 2 * HALF_STATES)

    ccri = jnp.stack([c_re, -c_im], axis=0).reshape(2, n_slab, gps, SSM_GROUP, p)
    cc = jnp.einsum('rsghp,gk->skrpgh', ccri, eye)
    cc = cc.reshape(n_slab, 2, gph, 2, p, gps, SSM_GROUP).transpose(0, 1, 3, 2, 4, 5, 6)
    cc = cc.reshape(n_slab, 2, 2 * HALF_STATES, SLAB).transpose(0, 2, 1, 3).reshape(n_slab, 2 * HALF_STATES, 2 * SLAB)

    def rows(a):
        a = a.reshape(n_slab, 2, 1, HALF_STATES)
        return jnp.broadcast_to(a, (n_slab, 2, 4, HALF_STATES)).reshape(n_slab, SUBLANES, HALF_STATES)

    return bb.astype(BF16), cc.astype(BF16), rows(ab_re), rows(ab_im)


def _ssm_kernel(u_ref, bb_ref, cc_ref, are_ref, aim_ref, d_ref, y_ref, x_scr, carry_scr, *, t_chunk):
    c = pl.program_id(0)
    s = pl.program_id(1)
    hs = HALF_STATES
    n_cg = 2 * hs // LANES
    n_re = hs // LANES
    half_rows = SUBLANES // 2


    @pl.when(c == 0)
    def _():
        carry_scr[s] = jnp.zeros((n_cg, SUBLANES, LANES), F32)

    a_re = [are_ref[0, :, g * LANES:(g + 1) * LANES] for g in range(n_re)]
    a_im = [aim_ref[0, :, g * LANES:(g + 1) * LANES] for g in range(n_re)]
    sub = t_chunk // SCAN_SUBCHUNKS

    def project_in(k):
        part = slice(k * sub * SUBLANES, (k + 1) * sub * SUBLANES)
        u4 = u_ref[k * sub:(k + 1) * sub]
        u3 = jnp.concatenate([u4, jnp.zeros_like(u4)], axis=1)
        u_lo = u3.reshape(sub * SUBLANES, SLAB)
        u_hi = pltpu.roll(u3, half_rows, 1).reshape(sub * SUBLANES, SLAB)
        lhs = jnp.concatenate([u_lo, u_hi], axis=1).astype(BF16)
        bu = jnp.dot(lhs, bb_ref[0], preferred_element_type=F32)
        for g in range(n_cg):
            x_scr[g, part, :] = bu[:, g * LANES:(g + 1) * LANES]

    def scan(k, x):
        for t in range(k * sub, (k + 1) * sub):
            idx = slice(t * SUBLANES, (t + 1) * SUBLANES)
            new = [None] * n_cg
            for g in range(n_re):
                x_re, x_im = x[g], x[n_re + g]
                new[g] = a_re[g] * x_re - a_im[g] * x_im + x_scr[g, idx, :]
                new[n_re + g] = a_re[g] * x_im + a_im[g] * x_re + x_scr[n_re + g, idx, :]
            for g in range(n_cg):
                x_scr[g, idx, :] = new[g]
            x = new
        return x

    def project_out(k):
        part = slice(k * sub * SUBLANES, (k + 1) * sub * SUBLANES)
        steps = slice(k * sub, (k + 1) * sub)
        xs = jnp.concatenate([x_scr[g, part, :] for g in range(n_cg)], axis=1).astype(BF16)
        yy = jnp.dot(xs, cc_ref[0], preferred_element_type=F32)
        y_half1 = pltpu.roll(yy[:, SLAB:].reshape(sub, SUBLANES, SLAB), half_rows, 1)
        y3 = yy[:, :SLAB].reshape(sub, SUBLANES, SLAB) + y_half1
        y_ref[steps] = y3[:, 0:half_rows, :] + d_ref[...] * u_ref[steps]

    x = [carry_scr[s, g] for g in range(n_cg)]
    project_in(0)
    for k in range(SCAN_SUBCHUNKS):
        if k + 1 < SCAN_SUBCHUNKS:
            project_in(k + 1)
        x = scan(k, x)
        if k > 0:
            project_out(k - 1)
    project_out(SCAN_SUBCHUNKS - 1)
    for g in range(n_cg):
        carry_scr[s, g] = x[g]


def _ssm(u, bb, cc, a_re, a_im, d_skip, t_chunk=512):
    s, r, width = u.shape
    assert r == SUBLANES // 2
    n_slab = width // SLAB
    hs = HALF_STATES
    return pl.pallas_call(
        functools.partial(_ssm_kernel, t_chunk=t_chunk),
        grid=(s // t_chunk, n_slab),
        in_specs=[pl.BlockSpec((t_chunk, SUBLANES // 2, SLAB), lambda c, sl: (c, 0, sl)),
                  pl.BlockSpec((1, 2 * SLAB, 2 * hs), lambda c, sl: (sl, 0, 0)),
                  pl.BlockSpec((1, 2 * hs, 2 * SLAB), lambda c, sl: (sl, 0, 0)),
                  pl.BlockSpec((1, SUBLANES, hs), lambda c, sl: (sl, 0, 0)),
                  pl.BlockSpec((1, SUBLANES, hs), lambda c, sl: (sl, 0, 0)),
                  pl.BlockSpec((1, SLAB), lambda c, sl: (0, sl))],
        out_specs=pl.BlockSpec((t_chunk, SUBLANES // 2, SLAB), lambda c, sl: (c, 0, sl)),
        out_shape=jax.ShapeDtypeStruct((s, SUBLANES // 2, width), F32),
        scratch_shapes=[pltpu.VMEM((2 * hs // LANES, SUBLANES * t_chunk, LANES), F32),
                        pltpu.VMEM((n_slab, 2 * hs // LANES, SUBLANES, LANES), F32)],
        compiler_params=_params(("arbitrary", "arbitrary")),
        name="s5_scan",
    )(u, bb, cc, a_re, a_im, d_skip.reshape(1, width))


def _mix_kernel(a0_ref, a1_ref, y_ref, x_ref, mod_ref, wglu_ref, bglu_ref, ga_ref, gs_ref, wo_ref, lng_ref, lnb_ref,
                o_ref):
    aw = a0_ref.shape[1]
    a = _rms_norm_rows(jnp.concatenate([a0_ref[...], a1_ref[...]], axis=0), ga_ref[...])
    y = jax.nn.gelu(y_ref[...])
    z = y * jax.nn.sigmoid(jnp.dot(y.astype(BF16), wglu_ref[...], preferred_element_type=F32) + bglu_ref[...])
    sn = _rms_norm_rows(z, gs_ref[...])
    mix = (jnp.dot(a.astype(BF16), wo_ref[0:aw, :], preferred_element_type=F32)
           + jnp.dot(sn.astype(BF16), wo_ref[aw:, :], preferred_element_type=F32))
    gate = mod_ref[0, 5:6, :]
    o_ref[...] = _residual_layer_norm(x_ref[...], mix, 1.0 + gate, lng_ref[...], lnb_ref[...])


def _mix(attn, y_rows, x2d, mod, w_glu, b_glu, g_attn, g_ssm, w_out, ln_g, ln_b, *, seq):
    n, d = x2d.shape
    _, _, n_steps, blk, aw = attn.shape
    tm = 2 * blk
    sw = y_rows.shape[1] // (n // seq)
    tiles_per_batch = seq // tm
    const = lambda m: (0, 0)

    def attn_spec(e):
        def index(m):
            i = 2 * (m % tiles_per_batch) + e
            folded = i >= n_steps
            return (m // tiles_per_batch, folded.astype(jnp.int32), jnp.where(folded, 2 * n_steps - 1 - i, i), 0, 0)
        return pl.BlockSpec((None, None, None, blk, aw), index)

    return pl.pallas_call(
        _mix_kernel,
        grid=(n // tm,),
        in_specs=[attn_spec(0), attn_spec(1),
                  pl.BlockSpec((tm, sw), lambda m: (m % tiles_per_batch, m // tiles_per_batch)),
                  pl.BlockSpec((tm, d), lambda m: (m, 0)),
                  pl.BlockSpec((1, N_MOD, d), lambda m: (m // tiles_per_batch, 0, 0)),
                  pl.BlockSpec((sw, sw), const),
                  pl.BlockSpec((1, sw), const),
                  pl.BlockSpec((1, aw), const),
                  pl.BlockSpec((1, sw), const),
                  pl.BlockSpec((aw + sw, d), const),
                  pl.BlockSpec((1, d), const),
                  pl.BlockSpec((1, d), const)],
        out_specs=pl.BlockSpec((tm, d), lambda m: (m, 0)),
        out_shape=jax.ShapeDtypeStruct((n, d), F32),
        compiler_params=_params(("parallel",)),
        name="mix",
    )(attn, attn, y_rows, x2d, mod, w_glu, b_glu.reshape(1, sw), g_attn.reshape(1, aw), g_ssm.reshape(1, sw),
      w_out, ln_g.reshape(1, d), ln_b.reshape(1, d))


def kernel(x, c, w_ada, b_ada, ffn1_w_gate, ffn1_w_up, ffn1_w_down, ln1_g, ln1_b, w_in, attn_norm_g, ssm_lambda_re, ssm_lambda_im, ssm_log_dt, ssm_b_re, ssm_b_im, ssm_c_re, ssm_c_im, ssm_d, ssm_w_glu, ssm_b_glu, ssm_norm_g, w_out, ln2_g, ln2_b, ffn2_w_gate, ffn2_w_up, ffn2_w_down, ln3_g, ln3_b):
    b, s, d = x.shape
    assert w_ada.shape[0] == DEPTH
    slopes = 2.0 ** (-(8.0 / N_HEADS) * jnp.arange(1, N_HEADS + 1, dtype=F32))
    c_pad = jnp.pad(c, ((0, SUBLANES - b), (0, 0)))
    x2d = x.reshape(b * s, d)
    for l in range(DEPTH):
        mod = _adaln(c_pad, w_ada[l], b_ada[l])[:b].reshape(b, N_MOD, d)

        x2d, (w_in_bf, w_out_bf, w_glu_bf) = _ffn(
            x2d, mod, ffn1_w_gate[l].astype(BF16), ffn1_w_up[l].astype(BF16), ffn1_w_down[l].astype(BF16),
            ln1_g[l], ln1_b[l], mod_base=0, seq=s, cast_jobs=(w_in[l], w_out[l], ssm_w_glu[l]))

        q, k, v, u = _inproj(x2d.reshape(b, s, d), mod, w_in_bf)
        attn, (w2_gate, w2_up, w2_down) = _attention(
            q, k, v, slopes, cast_jobs=(ffn2_w_gate[l], ffn2_w_up[l], ffn2_w_down[l]))

        bb, cc, a_re, a_im = _ssm_tables(ssm_lambda_re[l], ssm_lambda_im[l], ssm_log_dt[l],
                                         ssm_b_re[l], ssm_b_im[l], ssm_c_re[l], ssm_c_im[l])
        sw = u.shape[2]
        y = _ssm(u, bb, cc, a_re, a_im, ssm_d[l])

        x2d = _mix(attn, y.reshape(s, b * sw), x2d, mod, w_glu_bf, ssm_b_glu[l],
                   attn_norm_g[l], ssm_norm_g[l], w_out_bf, ln2_g[l], ln2_b[l], seq=s)

        x2d, _ = _ffn(x2d, mod, w2_gate, w2_up, w2_down, ln3_g[l], ln3_b[l], mod_base=6, seq=s)
    return x2d.reshape(b, s, d)
```
